```python
import jax, jax.numpy as jnp
from jax import lax
import numpy as np

D_MODEL = 1024
BATCH = 8
SEQ = 4096
DEPTH = 4

HEAD_DIM = 64
MLA_HEADS = 8
MLA_Q_RANK = 192
MLA_KV_RANK = 128
MLA_NOPE = 64
MLA_ROPE = 32
MLA_V = 64
ROPE_THETA = 10000.0
FOX_HEADS = 8
DSA_HEADS = 8
IDX_HEADS = 8
IDX_DIM = 32
DSA_MAX_TOPK = 256
N_BRANCH = 3
BRANCH_WIDTH = MLA_HEADS * MLA_V
D_FF = 2816
CONV_WIDTH = 3
Q_BLOCK = 128
LN_EPS = 1e-5
RMS_EPS = 1e-6
DEEPNORM_ALPHA = (2 * DEPTH) ** 0.25
DEEPNORM_BETA = (8 * DEPTH) ** -0.25

SPLIT_SIZES = (
    MLA_Q_RANK, MLA_KV_RANK, MLA_ROPE,
    FOX_HEADS * HEAD_DIM, FOX_HEADS * HEAD_DIM, FOX_HEADS * HEAD_DIM, FOX_HEADS,
    DSA_HEADS * HEAD_DIM, HEAD_DIM, HEAD_DIM,
    IDX_HEADS * IDX_DIM, IDX_DIM, IDX_HEADS,
    N_BRANCH * D_MODEL,
)
D_IN = sum(SPLIT_SIZES)

kernel_name = 'hybrid_mla_fox_dsa_deepnorm'


def _layernorm(x, g, b):
    xf = x.astype(jnp.float32)
    mu = jnp.mean(xf, -1, keepdims=True)
    var = jnp.mean(jnp.square(xf - mu), -1, keepdims=True)
    return ((xf - mu) * lax.rsqrt(var + LN_EPS) * g + b).astype(x.dtype)


def _rmsnorm(x, g):
    xf = x.astype(jnp.float32)
    return (xf * lax.rsqrt(jnp.mean(jnp.square(xf), -1, keepdims=True) + RMS_EPS) * g).astype(x.dtype)


def _split_cols(z):
    parts = []
    off = 0
    for width in SPLIT_SIZES:
        parts.append(z[..., off:off + width])
        off += width
    return parts


def _rope(x, pos):
    half = x.shape[-1] // 2
    inv = ROPE_THETA ** (-jnp.arange(half, dtype=jnp.float32) / half)
    ang = pos.astype(jnp.float32)[..., None] * inv
    cos = jnp.cos(ang)[:, :, None, :]
    sin = jnp.sin(ang)[:, :, None, :]
    xf = x.astype(jnp.float32)
    x1, x2 = xf[..., :half], xf[..., half:]
    return jnp.concatenate([x1 * cos - x2 * sin, x1 * sin + x2 * cos], -1).astype(x.dtype)


def _alibi_slopes(n):
    return jnp.exp2(-8.0 * jnp.arange(1, n + 1, dtype=jnp.float32) / n)


def _sweep_blocks(block_fn, seq):
    starts = jnp.arange(seq // Q_BLOCK, dtype=jnp.int32) * Q_BLOCK
    out = lax.map(block_fn, starts)
    nb, b, qb, h, dv = out.shape
    return jnp.transpose(out, (1, 0, 2, 3, 4)).reshape(b, nb * qb, h * dv)


def _dense_causal_attention(q, k, v, scale, log_decay_cum):
    seq = q.shape[1]
    key_pos = jnp.arange(seq, dtype=jnp.int32)
    cum_k = None if log_decay_cum is None else jnp.transpose(log_decay_cum, (0, 2, 1))

    def block(start):
        qb = lax.dynamic_slice_in_dim(q, start, Q_BLOCK, axis=1)
        s = jnp.einsum('bqhd,bkhd->bhqk', qb, k).astype(jnp.float32) * scale
        if cum_k is not None:
            cum_q = lax.dynamic_slice_in_dim(cum_k, start, Q_BLOCK, axis=2)
            s = s + (cum_q[..., :, None] - cum_k[..., None, :])
        q_pos = start + jnp.arange(Q_BLOCK, dtype=jnp.int32)
        s = jnp.where(key_pos[None, :] <= q_pos[:, None], s, -jnp.inf)
        p = jax.nn.softmax(s, axis=-1).astype(v.dtype)
        return jnp.einsum('bhqk,bkhd->bqhd', p, v)

    return _sweep_blocks(block, seq)


def _dsa_attention(q, k_sh, v_sh, iq, ik, iw, positions, slopes, topk):
    seq = q.shape[1]
    key_pos = jnp.arange(seq, dtype=jnp.int32)
    scale = HEAD_DIM ** -0.5
    gather = jax.vmap(lambda table, idx: table[idx])

    def block(start):
        qb = lax.dynamic_slice_in_dim(q, start, Q_BLOCK, axis=1)
        iqb = lax.dynamic_slice_in_dim(iq, start, Q_BLOCK, axis=1)
        iwb = lax.dynamic_slice_in_dim(iw, start, Q_BLOCK, axis=1).astype(jnp.float32)
        pq = lax.dynamic_slice_in_dim(positions, start, Q_BLOCK, axis=1)
        q_pos = start + jnp.arange(Q_BLOCK, dtype=jnp.int32)
        causal = key_pos[None, :] <= q_pos[:, None]
        idx_logit = jnp.einsum('bqhe,bke->bqhk', iqb, ik).astype(jnp.float32)
        score = jnp.einsum('bqhk,bqh->bqk', jax.nn.relu(idx_logit), iwb)
        score = jnp.where(causal[None], score, -jnp.inf)
        _, sel = lax.top_k(score, topk)
        k_sel = gather(k_sh, sel)
        v_sel = gather(v_sh, sel)
        p_sel = gather(positions, sel)
        s = jnp.einsum('bqhd,bqkd->bhqk', qb, k_sel).astype(jnp.float32) * scale
        dist = jnp.abs(pq[:, :, None] - p_sel).astype(jnp.float32)
        s = s - slopes[None, :, None, None] * dist[:, None]
        valid = sel <= q_pos[None, :, None]
        s = jnp.where(valid[:, None], s, -jnp.inf)
        p = jax.nn.softmax(s, axis=-1).astype(v_sh.dtype)
        return jnp.einsum('bhqk,bqkd->bqhd', p, v_sel)

    return _sweep_blocks(block, seq)


def _mixer(x, positions, w_in, b_gate, b_forget, q_norm, kv_norm, w_uq, w_ukv, w_branch, w_out, slopes, topk):
    bsz, seq, _ = x.shape
    z = x @ w_in
    (c_q, c_kv, k_rope, f_q, f_k, f_v, f_logit, s_q, s_k, s_v, i_q, i_k, i_w, gate_logit) = _split_cols(z)
    q = (_rmsnorm(c_q, q_norm) @ w_uq).reshape(bsz, seq, MLA_HEADS, MLA_NOPE + MLA_ROPE)
    kv = (_rmsnorm(c_kv, kv_norm) @ w_ukv).reshape(bsz, seq, MLA_HEADS, MLA_NOPE + MLA_V)
    q = jnp.concatenate([q[..., :MLA_NOPE], _rope(q[..., MLA_NOPE:], positions)], -1)
    k_r = jnp.broadcast_to(_rope(k_rope[:, :, None, :], positions), (bsz, seq, MLA_HEADS, MLA_ROPE))
    k = jnp.concatenate([kv[..., :MLA_NOPE], k_r], -1)
    o_mla = _dense_causal_attention(q, k, kv[..., MLA_NOPE:], (MLA_NOPE + MLA_ROPE) ** -0.5, None)
    log_f = jax.nn.log_sigmoid((f_logit + b_forget).astype(jnp.float32))
    heads = lambda t, h: t.reshape(bsz, seq, h, HEAD_DIM)
    o_fox = _dense_causal_attention(heads(f_q, FOX_HEADS), heads(f_k, FOX_HEADS), heads(f_v, FOX_HEADS),
                                    HEAD_DIM ** -0.5, jnp.cumsum(log_f, axis=1))
    o_dsa = _dsa_attention(heads(s_q, DSA_HEADS), s_k, s_v, i_q.reshape(bsz, seq, IDX_HEADS, IDX_DIM),
                           i_k, i_w, positions, slopes, topk)
    branches = jnp.stack([o_mla, o_fox, o_dsa], axis=2)
    proj = jnp.einsum('bsnc,ncd->bsnd', branches, w_branch)
    gates = jax.nn.sigmoid(gate_logit.reshape(bsz, seq, N_BRANCH, D_MODEL) + b_gate)
    merged = jnp.einsum('bsnd,bsnd->bsd', gates, proj)
    return merged @ w_out


def _conv_glu_ffn(x, w_up, conv_w, conv_b, w_down):
    seq = x.shape[1]
    u = x @ w_up
    gate, val = u[..., :D_FF], u[..., D_FF:]
    gp = jnp.pad(gate, ((0, 0), (CONV_WIDTH - 1, 0), (0, 0)))
    conv = conv_b
    for j in range(CONV_WIDTH):
        conv = conv + gp[:, j:j + seq] * conv_w[j]
    return (jax.nn.gelu(conv, approximate=False) * val) @ w_down


def setup_inputs(seed: int = 0) -> dict:
    key = jax.random.key(seed)
    ks = jax.random.split(key, 20)
    f32 = jnp.float32
    nrm = lambda k, shape, scale: jax.random.normal(k, shape, f32) * scale
    x = nrm(ks[0], (BATCH, SEQ, D_MODEL), 1.0)
    start = jax.random.randint(ks[1], (BATCH, 1), 0, 1024, dtype=jnp.int32)
    positions = start + jnp.arange(SEQ, dtype=jnp.int32)[None, :]
    w_in = nrm(ks[2], (DEPTH, D_MODEL, D_IN), D_MODEL ** -0.5)
    b_gate = nrm(ks[3], (DEPTH, N_BRANCH, D_MODEL), 0.02)
    b_forget = 1.0 + 3.0 * jax.random.uniform(ks[4], (DEPTH, FOX_HEADS), f32)
    mla_q_norm = 1.0 + nrm(ks[5], (DEPTH, MLA_Q_RANK), 0.02)
    mla_kv_norm = 1.0 + nrm(ks[6], (DEPTH, MLA_KV_RANK), 0.02)
    mla_w_uq = nrm(ks[7], (DEPTH, MLA_Q_RANK, MLA_HEADS * (MLA_NOPE + MLA_ROPE)), MLA_Q_RANK ** -0.5)
    mla_w_ukv = nrm(ks[8], (DEPTH, MLA_KV_RANK, MLA_HEADS * (MLA_NOPE + MLA_V)), MLA_KV_RANK ** -0.5)
    w_branch = nrm(ks[9], (DEPTH, N_BRANCH, BRANCH_WIDTH, D_MODEL), DEEPNORM_BETA * BRANCH_WIDTH ** -0.5)
    w_out = nrm(ks[10], (DEPTH, D_MODEL, D_MODEL), DEEPNORM_BETA * D_MODEL ** -0.5)
    ln1_g = 1.0 + nrm(ks[11], (DEPTH, D_MODEL), 0.02)
    ln1_b = nrm(ks[12], (DEPTH, D_MODEL), 0.02)
    ffn_w_up = nrm(ks[13], (DEPTH, D_MODEL, 2 * D_FF), D_MODEL ** -0.5)
    ffn_conv_w = nrm(ks[14], (DEPTH, CONV_WIDTH, D_FF), CONV_WIDTH ** -0.5)
    ffn_conv_b = nrm(ks[15], (DEPTH, D_FF), 0.02)
    ffn_w_down = nrm(ks[16], (DEPTH, D_FF, D_MODEL), DEEPNORM_BETA * D_FF ** -0.5)
    ln2_g = 1.0 + nrm(ks[17], (DEPTH, D_MODEL), 0.02)
    ln2_b = nrm(ks[18], (DEPTH, D_MODEL), 0.02)
    return {'x': x, 'positions': positions, 'w_in': w_in, 'b_gate': b_gate, 'b_forget': b_forget,
            'mla_q_norm': mla_q_norm, 'mla_kv_norm': mla_kv_norm, 'mla_w_uq': mla_w_uq, 'mla_w_ukv': mla_w_ukv,
            'w_branch': w_branch, 'w_out': w_out, 'ln1_g': ln1_g, 'ln1_b': ln1_b,
            'ffn_w_up': ffn_w_up, 'ffn_conv_w': ffn_conv_w, 'ffn_conv_b': ffn_conv_b, 'ffn_w_down': ffn_w_down,
            'ln2_g': ln2_g, 'ln2_b': ln2_b}


def reference(x, positions, w_in, b_gate, b_forget, mla_q_norm, mla_kv_norm, mla_w_uq, mla_w_ukv,
              w_branch, w_out, ln1_g, ln1_b, ffn_w_up, ffn_conv_w, ffn_conv_b, ffn_w_down, ln2_g, ln2_b):
    seq = x.shape[1]
    topk = min(DSA_MAX_TOPK, seq // 4)
    slopes = _alibi_slopes(DSA_HEADS)
    for l in range(DEPTH):
        y = _mixer(x, positions, w_in[l], b_gate[l], b_forget[l], mla_q_norm[l], mla_kv_norm[l],
                   mla_w_uq[l], mla_w_ukv[l], w_branch[l], w_out[l], slopes, topk)
        x = _layernorm(DEEPNORM_ALPHA * x + y, ln1_g[l], ln1_b[l])
        y = _conv_glu_ffn(x, ffn_w_up[l], ffn_conv_w[l], ffn_conv_b[l], ffn_w_down[l])
        x = _layernorm(DEEPNORM_ALPHA * x + y, ln2_g[l], ln2_b[l])
    return x
```

```python
import functools
import math

import jax
import jax.numpy as jnp
from jax import lax
from jax.experimental import pallas as pl
from jax.experimental.pallas import tpu as pltpu

F32 = jnp.float32
BF16 = jnp.bfloat16

N_HEADS = 8
HEAD_DIM = 64
MLA_Q_RANK = 192
MLA_KV_RANK = 128
MLA_NOPE = 64
MLA_ROPE = 32
MLA_QK = MLA_NOPE + MLA_ROPE
IDX_DIM = 32
N_BRANCH = 3
CONV_WIDTH = 3
ROPE_THETA = 10000.0
DSA_MAX_TOPK = 256
LN_EPS = 1e-5
RMS_EPS = 1e-6
LOG2E = math.log2(math.e)

ZA_W = 384
ZB_OFF, ZB_W = 384, 1664
ZC_OFF, ZC_W = 2048, 1024
ZG_OFF = 3072
FOX_QK = 80

ATT_TQ = 256
ATT_TK = 256
MASKED = -1e30
M_INIT = -5e29
INT_MIN = -2 ** 31
FF_CHUNK = 256
VMEM_LIMIT = 56 * 1024 * 1024


def _cparams(sem):
    return pltpu.CompilerParams(dimension_semantics=sem, vmem_limit_bytes=VMEM_LIMIT)


def _mm_kernel(x_ref, w_ref, o_ref):
    o_ref[...] = jnp.dot(x_ref[...], w_ref[...], preferred_element_type=F32).astype(o_ref.dtype)


def _matmul(x, w, out_dtype, tm, tn):
    m, k = x.shape
    n = w.shape[1]
    return pl.pallas_call(
        _mm_kernel,
        grid=(n // tn, m // tm),
        in_specs=[pl.BlockSpec((tm, k), lambda j, i: (i, 0)),
                  pl.BlockSpec((k, tn), lambda j, i: (0, j))],
        out_specs=pl.BlockSpec((tm, tn), lambda j, i: (i, j)),
        out_shape=jax.ShapeDtypeStruct((m, n), out_dtype),
        compiler_params=_cparams(("parallel", "parallel")),
        name="in_proj",
    )(x, w)


def _mla_prep_kernel(z_ref, cq_ref, sq_ref, ck_ref, sk_ref, qn_ref, kvn_ref, wq_ref, wqr_ref,
                     wk_ref, wv_ref, e_ref, q_out, k_out, v_out):
    z = z_ref[...]
    c_kv = z[:, 0:MLA_KV_RANK]
    c_q = z[:, MLA_KV_RANK:MLA_KV_RANK + MLA_Q_RANK]
    kr = z[:, 320:352]
    kr_rot = z[:, 352:384]
    nq = (c_q * lax.rsqrt(jnp.mean(jnp.square(c_q), -1, keepdims=True) + RMS_EPS) * qn_ref[...]).astype(BF16)
    nkv = (c_kv * lax.rsqrt(jnp.mean(jnp.square(c_kv), -1, keepdims=True) + RMS_EPS) * kvn_ref[...]).astype(BF16)
    q = jnp.dot(nq, wq_ref[...], preferred_element_type=F32)
    q_rot = jnp.dot(nq, wqr_ref[...], preferred_element_type=F32)
    q_out[...] = (q * cq_ref[...] + q_rot * sq_ref[...]).astype(BF16)
    k_rope = (kr * ck_ref[...] + kr_rot * sk_ref[...]).astype(BF16)
    k = jnp.dot(nkv, wk_ref[...], preferred_element_type=F32) + jnp.dot(k_rope, e_ref[...], preferred_element_type=F32)
    k_out[...] = k.astype(BF16)
    v_out[...] = jnp.dot(nkv, wv_ref[...], preferred_element_type=F32).astype(BF16)


def _mla_prep(z, cq_tab, sq_tab, ck_tab, sk_tab, qn, kvn, wq, wqr, wk, wv, e, tm):
    m = z.shape[0]
    row = lambda w: pl.BlockSpec((tm, w), lambda i: (i, 0))
    full = lambda a: pl.BlockSpec(a.shape, lambda i: (0,) * a.ndim)
    return pl.pallas_call(
        _mla_prep_kernel,
        grid=(m // tm,),
        in_specs=[row(ZA_W), row(N_HEADS * MLA_QK), row(N_HEADS * MLA_QK), row(MLA_ROPE), row(MLA_ROPE),
                  full(qn), full(kvn), full(wq), full(wqr), full(wk), full(wv), full(e)],
        out_specs=[row(N_HEADS * MLA_QK), row(N_HEADS * MLA_QK), row(N_HEADS * HEAD_DIM)],
        out_shape=[jax.ShapeDtypeStruct((m, N_HEADS * MLA_QK), BF16),
                   jax.ShapeDtypeStruct((m, N_HEADS * MLA_QK), BF16),
                   jax.ShapeDtypeStruct((m, N_HEADS * HEAD_DIM), BF16)],
        compiler_params=_cparams(("parallel",)),
        name="mla_prep",
    )(z, cq_tab, sq_tab, ck_tab, sk_tab, qn, kvn, wq, wqr, wk, wv, e)


def _fox_cum_kernel(x_ref, b_ref, tri_ref, hi_ref, mid_ref, lo_ref, *, chunk):
    seq = x_ref.shape[2]
    lf = jax.nn.log_sigmoid(x_ref[0] + b_ref[...])
    carry = jnp.zeros((N_HEADS, 1), F32)
    for c in range(seq // chunk):
        seg = lf[:, c * chunk:(c + 1) * chunk]
        cs = jnp.dot(seg, tri_ref[...], preferred_element_type=F32, precision=lax.Precision.HIGHEST) + carry
        carry = cs[:, chunk - 1:chunk]
        c2 = cs * LOG2E
        hi = c2.astype(BF16)
        r1 = c2 - hi.astype(F32)
        mid = r1.astype(BF16)
        lo = (r1 - mid.astype(F32)).astype(BF16)
        hi_ref[0, :, c * chunk:(c + 1) * chunk] = hi.astype(F32)
        mid_ref[0, :, c * chunk:(c + 1) * chunk] = mid.astype(F32)
        lo_ref[0, :, c * chunk:(c + 1) * chunk] = lo.astype(F32)


def _fox_cum(logit_t, b_forget):
    b, h, s = logit_t.shape
    chunk = 256
    tri = (jnp.arange(chunk)[:, None] <= jnp.arange(chunk)[None, :]).astype(F32)
    spec = pl.BlockSpec((1, h, s), lambda i: (i, 0, 0))
    return pl.pallas_call(
        functools.partial(_fox_cum_kernel, chunk=chunk),
        grid=(b,),
        in_specs=[spec, pl.BlockSpec((h, 1), lambda i: (0, 0)), pl.BlockSpec((chunk, chunk), lambda i: (0, 0))],
        out_specs=[spec, spec, spec],
        out_shape=[jax.ShapeDtypeStruct((b, h, s), F32)] * 3,
        compiler_params=_cparams(("parallel",)),
        name="fox_cum",
    )(logit_t, b_forget.reshape(h, 1), tri)


def _softmax_step(s, v_t, m_ref, l_ref, acc_ref, h):
    m_old = m_ref[h]
    m_new = jnp.maximum(m_old, jnp.max(s, axis=0, keepdims=True))
    p = jnp.exp2(s - m_new)
    alpha = jnp.exp2(m_old - m_new)
    l_ref[h] = alpha * l_ref[h] + jnp.sum(p, axis=0, keepdims=True)
    acc_ref[h] = alpha * acc_ref[h] + jnp.dot(v_t, p.astype(BF16), preferred_element_type=F32)
    m_ref[h] = m_new


def _causal_mask(tk, tq):
    return lax.broadcasted_iota(jnp.int32, (tk, tq), 0) <= lax.broadcasted_iota(jnp.int32, (tk, tq), 1)


def _attn_kernel(qt_ref, k_ref, vt_ref, o_ref, m_ref, l_ref, acc_ref):
    j = pl.program_id(1)
    tk, tq = k_ref.shape[3], qt_ref.shape[3]
    m_ref[...] = jnp.full(m_ref.shape, M_INIT, F32)
    l_ref[...] = jnp.zeros(l_ref.shape, F32)
    acc_ref[...] = jnp.zeros(acc_ref.shape, F32)

    def tile(i, causal):
        for h in range(N_HEADS):
            s = jnp.dot(k_ref[0, h, i], qt_ref[0, h], preferred_element_type=F32)
            if causal is not None:
                s = jnp.where(causal, s, MASKED)
            _softmax_step(s, vt_ref[0, h, i], m_ref, l_ref, acc_ref, h)

    def body(i, carry):
        tile(i, None)
        return carry

    lax.fori_loop(0, j, body, 0)
    tile(j, _causal_mask(tk, tq))
    for h in range(N_HEADS):
        o_ref[0, h] = (acc_ref[h] / l_ref[h]).astype(o_ref.dtype)


def _attention(qt, k, vt):
    b, h, dk, s = qt.shape
    nk, tk = k.shape[2], k.shape[3]
    dv = vt.shape[3]
    tq = ATT_TQ
    return pl.pallas_call(
        _attn_kernel,
        grid=(b, s // tq),
        in_specs=[pl.BlockSpec((1, h, dk, tq), lambda bi, j: (bi, 0, 0, j)),
                  pl.BlockSpec((1, h, nk, tk, dk), lambda bi, j: (bi, 0, 0, 0, 0)),
                  pl.BlockSpec((1, h, nk, dv, tk), lambda bi, j: (bi, 0, 0, 0, 0))],
        out_specs=pl.BlockSpec((1, h, dv, tq), lambda bi, j: (bi, 0, 0, j)),
        out_shape=jax.ShapeDtypeStruct((b, h, dv, s), BF16),
        scratch_shapes=[pltpu.VMEM((h, 1, tq), F32), pltpu.VMEM((h, 1, tq), F32), pltpu.VMEM((h, dv, tq), F32)],
        compiler_params=_cparams(("parallel", "arbitrary")),
        name="flash_attn",
    )(qt, k, vt)


def _sortable_key(x):
    bits = lax.bitcast_convert_type(x, jnp.int32)
    return bits ^ ((bits >> 31) & 0x7FFFFFFF)


def _count_ge(keys_ref, n_tiles, cand, pred=None):
    tk, tq = keys_ref.shape[1], keys_ref.shape[2]

    def body(i, c):
        hit = keys_ref[i] >= cand
        return c + jnp.sum(hit.astype(jnp.int32).reshape(tk // 8, 8, tq), axis=0)

    c = lax.fori_loop(0, n_tiles, body, jnp.zeros((8, tq), jnp.int32))
    return jnp.sum(c, axis=0, keepdims=True)


def _dsa_kernel(sqt_ref, sk_ref, svt_ref, iqt_ref, ik_ref, iw_ref, prow_ref, pcol_ref, o_ref,
                keys_ref, m_ref, l_ref, acc_ref, *, topk, slopes):
    j = pl.program_id(1)
    tk, tq = sk_ref.shape[2], sqt_ref.shape[3]
    n_tiles = j + 1
    causal = _causal_mask(tk, tq)

    def score_tile(i, diag):
        ik = ik_ref[0, i]
        sc = jnp.zeros((tk, tq), F32)
        for h in range(N_HEADS):
            lg = jnp.dot(ik, iqt_ref[0, h], preferred_element_type=F32)
            sc = sc + iw_ref[0, h:h + 1, :] * jnp.maximum(lg, 0.0)
        key = _sortable_key(sc)
        if diag:
            key = jnp.where(causal, key, INT_MIN)
        keys_ref[i] = key

    def score_body(i, carry):
        score_tile(i, False)
        return carry

    lax.fori_loop(0, j, score_body, 0)
    score_tile(j, True)

    cnt = _count_ge(keys_ref, n_tiles, jnp.zeros((1, tq), jnp.int32))
    prefix = jnp.where(cnt >= topk, jnp.int32(0), jnp.int32(INT_MIN))

    def bit_body(it, prefix):
        cand = prefix + jnp.left_shift(jnp.int32(1), 30 - it)
        cnt = _count_ge(keys_ref, n_tiles, cand)
        return jnp.where(cnt >= topk, cand, prefix)

    thr = lax.fori_loop(0, 31, bit_body, prefix)

    real = thr > INT_MIN
    cnt_ge = _count_ge(keys_ref, n_tiles, thr)
    tie = jnp.logical_and(real, cnt_ge > topk)
    any_tie = jnp.max(tie.astype(jnp.int32)) > 0

    @pl.when(any_tie)
    def _():
        def count_eq_below(limit):
            def body(i, c):
                idx = i * tk + lax.broadcasted_iota(jnp.int32, (tk, tq), 0)
                hit = jnp.logical_and(keys_ref[i] == thr, idx < limit)
                return c + jnp.sum(hit.astype(jnp.int32).reshape(tk // 8, 8, tq), axis=0)
            c = lax.fori_loop(0, n_tiles, body, jnp.zeros((8, tq), jnp.int32))
            return jnp.sum(c, axis=0, keepdims=True)

        cnt_gt = cnt_ge - count_eq_below(jnp.full((1, tq), n_tiles * tk, jnp.int32))
        need = topk - cnt_gt
        n_bits = max(1, int(math.ceil(math.log2(keys_ref.shape[0] * tk))))

        def idx_body(it, lo):
            t = lo + jnp.left_shift(jnp.int32(1), n_bits - 1 - it)
            return jnp.where(count_eq_below(t) < need, t, lo)

        last = lax.fori_loop(0, n_bits, idx_body, jnp.zeros((1, tq), jnp.int32))

        def drop_body(i, carry):
            idx = i * tk + lax.broadcasted_iota(jnp.int32, (tk, tq), 0)
            k = keys_ref[i]
            drop = jnp.logical_and(jnp.logical_and(k == thr, idx > last), tie)
            keys_ref[i] = jnp.where(drop, INT_MIN, k)
            return carry

        lax.fori_loop(0, n_tiles, drop_body, 0)

    thr_eff = jnp.maximum(thr, INT_MIN + 1)

    m_ref[...] = jnp.full(m_ref.shape, M_INIT, F32)
    l_ref[...] = jnp.zeros(l_ref.shape, F32)
    acc_ref[...] = jnp.zeros(acc_ref.shape, F32)
    p_row = prow_ref[0]

    def attn_body(i, carry):
        sel = keys_ref[i] >= thr_eff
        dist = jnp.abs(p_row - pcol_ref[0, i]).astype(F32)
        k_t = sk_ref[0, i]
        v_t = svt_ref[0, i]
        for h in range(N_HEADS):
            s = jnp.dot(k_t, sqt_ref[0, h], preferred_element_type=F32) - (slopes[h] * LOG2E) * dist
            _softmax_step(jnp.where(sel, s, MASKED), v_t, m_ref, l_ref, acc_ref, h)
        return carry

    lax.fori_loop(0, n_tiles, attn_body, 0)
    for h in range(N_HEADS):
        o_ref[0, h] = (acc_ref[h] / l_ref[h]).astype(o_ref.dtype)


def _dsa(sqt, sk, svt, iqt, ik, iw_t, pos_row, pos_col, topk, slopes):
    b, h, d, s = sqt.shape
    nk, tk = sk.shape[1], sk.shape[2]
    tq = ATT_TQ
    kv = lambda a: pl.BlockSpec((1,) + a.shape[1:], lambda bi, j: (bi,) + (0,) * (a.ndim - 1))
    return pl.pallas_call(
        functools.partial(_dsa_kernel, topk=topk, slopes=slopes),
        grid=(b, s // tq),
        in_specs=[pl.BlockSpec((1, h, d, tq), lambda bi, j: (bi, 0, 0, j)),
                  kv(sk), kv(svt),
                  pl.BlockSpec((1, h, IDX_DIM, tq), lambda bi, j: (bi, 0, 0, j)),
                  kv(ik),
                  pl.BlockSpec((1, h, tq), lambda bi, j: (bi, 0, j)),
                  pl.BlockSpec((1, 1, tq), lambda bi, j: (bi, 0, j)),
                  kv(pos_col)],
        out_specs=pl.BlockSpec((1, h, d, tq), lambda bi, j: (bi, 0, 0, j)),
        out_shape=jax.ShapeDtypeStruct((b, h, d, s), BF16),
        scratch_shapes=[pltpu.VMEM((nk, tk, tq), jnp.int32),
                        pltpu.VMEM((h, 1, tq), F32), pltpu.VMEM((h, 1, tq), F32), pltpu.VMEM((h, d, tq), F32)],
        compiler_params=_cparams(("parallel", "arbitrary")),
        name="dsa",
    )(sqt, sk, svt, iqt, ik, iw_t, pos_row, pos_col)


def _layernorm(r, g, b):
    mu = jnp.mean(r, -1, keepdims=True)
    d = r - mu
    var = jnp.mean(jnp.square(d), -1, keepdims=True)
    return d * lax.rsqrt(var + LN_EPS) * g + b


def _merge_kernel(o1_ref, o2_ref, o3_ref, g_ref, x_ref, wb_ref, bg_ref, wo_ref, lng_ref, lnb_ref,
                  xo_ref, xb_ref, *, alpha):
    dm = x_ref.shape[1]
    merged = jnp.zeros(x_ref.shape, F32)
    for n, o_ref in enumerate((o1_ref, o2_ref, o3_ref)):
        proj = jnp.dot(o_ref[...], wb_ref[n], preferred_element_type=F32)
        gate = jax.nn.sigmoid(g_ref[:, n * dm:(n + 1) * dm] + bg_ref[n:n + 1, :])
        merged = merged + gate * proj
    y = jnp.dot(merged.astype(BF16), wo_ref[...], preferred_element_type=F32)
    out = _layernorm(alpha * x_ref[...] + y, lng_ref[...], lnb_ref[...])
    xo_ref[...] = out
    xb_ref[...] = out.astype(BF16)


def _merge(o1, o2, o3, z, x, wb, bg, wo, lng, lnb, alpha, tm):
    m, dm = x.shape
    bw = o1.shape[1]
    row = lambda w: pl.BlockSpec((tm, w), lambda i: (i, 0))
    full = lambda a: pl.BlockSpec(a.shape, lambda i: (0,) * a.ndim)
    return pl.pallas_call(
        functools.partial(_merge_kernel, alpha=alpha),
        grid=(m // tm,),
        in_specs=[row(bw), row(bw), row(bw),
                  pl.BlockSpec((tm, N_BRANCH * dm), lambda i: (i, ZG_OFF // (N_BRANCH * dm))),
                  row(dm), full(wb), full(bg), full(wo), full(lng), full(lnb)],
        out_specs=[row(dm), row(dm)],
        out_shape=[jax.ShapeDtypeStruct((m, dm), F32), jax.ShapeDtypeStruct((m, dm), BF16)],
        compiler_params=_cparams(("parallel",)),
        name="merge",
    )(o1, o2, o3, z, x, wb, bg, wo, lng, lnb)


def _ffn_kernel(x_ref, wg_ref, wv_ref, cw_ref, cb_ref, wd_ref, lng_ref, lnb_ref, xo_ref, xb_ref,
                tail_ref, y_ref, *, alpha):
    tm = x_ref.shape[1]
    n_chunks = wg_ref.shape[0]
    first = pl.program_id(1) == 0
    x = x_ref[0]
    xb = x.astype(BF16)
    row = lax.broadcasted_iota(jnp.int32, (tm, FF_CHUNK), 0)
    y_ref[...] = jnp.zeros(y_ref.shape, F32)

    @pl.when(first)
    def _():
        tail_ref[...] = jnp.zeros(tail_ref.shape, F32)

    def body(c, carry):
        g = jnp.dot(xb, wg_ref[c], preferred_element_type=F32)
        v = jnp.dot(xb, wv_ref[c], preferred_element_type=F32)
        tail = tail_ref[c]
        g1 = jnp.where(row < 1, tail[7:8, :], pltpu.roll(g, 1, axis=0))
        g2 = jnp.where(row < 2, jnp.where(row < 1, tail[6:7, :], tail[7:8, :]), pltpu.roll(g, 2, axis=0))
        cw = cw_ref[c]
        conv = cb_ref[c] + g2 * cw[0:1, :] + g1 * cw[1:2, :] + g * cw[2:3, :]
        gelu = 0.5 * conv * (1.0 + lax.erf(conv * (2.0 ** -0.5)))
        hid = (gelu * v).astype(BF16)
        y_ref[...] += jnp.dot(hid, wd_ref[c], preferred_element_type=F32)
        tail_ref[c] = g[tm - 8:tm, :]
        return carry

    lax.fori_loop(0, n_chunks, body, 0)
    out = _layernorm(alpha * x + y_ref[...], lng_ref[...], lnb_ref[...])
    xo_ref[0] = out
    xb_ref[0] = out.astype(BF16)


def _ffn(x, wg, wv, cw, cb, wd, lng, lnb, alpha, tm):
    b, s, dm = x.shape
    n_chunks = wg.shape[0]
    full = lambda a: pl.BlockSpec(a.shape, lambda bi, i: (0,) * a.ndim, pipeline_mode=pl.Buffered(1))
    blk = pl.BlockSpec((1, tm, dm), lambda bi, i: (bi, i, 0))
    return pl.pallas_call(
        functools.partial(_ffn_kernel, alpha=alpha),
        grid=(b, s // tm),
        in_specs=[blk, full(wg), full(wv), full(cw), full(cb), full(wd), full(lng), full(lnb)],
        out_specs=[blk, blk],
        out_shape=[jax.ShapeDtypeStruct((b, s, dm), F32), jax.ShapeDtypeStruct((b, s, dm), BF16)],
        scratch_shapes=[pltpu.VMEM((n_chunks, 8, FF_CHUNK), F32), pltpu.VMEM((tm, dm), F32)],
        compiler_params=_cparams(("parallel", "arbitrary")),
        name="ffn",
    )(x, wg, wv, cw, cb, wd, lng, lnb)


def _split_offsets(d_model):
    sizes = (MLA_Q_RANK, MLA_KV_RANK, MLA_ROPE, 512, 512, 512, N_HEADS, 512, HEAD_DIM, HEAD_DIM,
             N_HEADS * IDX_DIM, IDX_DIM, N_HEADS, N_BRANCH * d_model)
    offs, o = [], 0
    for w in sizes:
        offs.append((o, o + w))
        o += w
    return offs


def _in_proj_weight(w_in):
    d = w_in.shape[0]
    (c_q, c_kv, k_r, f_q, f_k, f_v, f_l, s_q, s_k, s_v, i_q, i_k, i_w, gate) = [
        w_in[:, a:b] for a, b in _split_offsets(d)]
    half = MLA_ROPE // 2
    k_r_rot = jnp.concatenate([-k_r[:, half:], k_r[:, :half]], axis=1)
    pad = lambda n: jnp.zeros((d, n), w_in.dtype)
    za = jnp.concatenate([c_kv, c_q, k_r, k_r_rot], axis=1)
    zb = jnp.concatenate([f_q, f_k, f_v, f_l, pad(ZB_W - 1544)], axis=1)
    zc = jnp.concatenate([s_q, s_k, s_v, i_q, i_k, i_w, pad(ZC_W - 936)], axis=1)
    return jnp.concatenate([za, zb, zc, gate], axis=1).astype(BF16)


def _mla_weights(w_uq, w_ukv):
    r = w_uq.shape[0]
    half = MLA_ROPE // 2
    wq = w_uq.reshape(r, N_HEADS, MLA_QK)
    rope = wq[:, :, MLA_NOPE:]
    wq_rot = jnp.concatenate([jnp.zeros((r, N_HEADS, MLA_NOPE), w_uq.dtype), -rope[:, :, half:], rope[:, :, :half]], axis=2)
    wkv = w_ukv.reshape(w_ukv.shape[0], N_HEADS, MLA_NOPE + HEAD_DIM)
    wk = jnp.concatenate([wkv[:, :, :MLA_NOPE], jnp.zeros((wkv.shape[0], N_HEADS, MLA_ROPE), w_ukv.dtype)], axis=2)
    wv = wkv[:, :, MLA_NOPE:]
    place = jnp.zeros((MLA_ROPE, N_HEADS, MLA_QK), F32)
    place = place.at[jnp.arange(MLA_ROPE), :, MLA_NOPE + jnp.arange(MLA_ROPE)].set(1.0)
    flat = lambda a: a.reshape(a.shape[0], -1).astype(BF16)
    return flat(w_uq), flat(wq_rot), flat(wk), flat(wv), flat(place)


def _rope_tables(positions):
    half = MLA_ROPE // 2
    inv = ROPE_THETA ** (-jnp.arange(half, dtype=F32) / half)
    ang = positions.astype(F32)[..., None] * inv
    cos, sin = jnp.cos(ang), jnp.sin(ang)
    b, s = positions.shape
    q_scale = (MLA_QK ** -0.5) * LOG2E
    cos2 = jnp.concatenate([cos, cos], -1)
    sin2 = jnp.concatenate([sin, sin], -1)
    ones = jnp.ones((b, s, MLA_NOPE), F32)
    cq = jnp.tile(jnp.concatenate([ones, cos2], -1) * q_scale, (1, 1, N_HEADS))
    sq = jnp.tile(jnp.concatenate([0.0 * ones, sin2], -1) * q_scale, (1, 1, N_HEADS))
    flat = lambda a: a.reshape(b * s, a.shape[-1])
    return flat(cq), flat(sq), flat(cos2), flat(sin2)


def _heads_t(a, b, s, h, d):
    return a.reshape(b, s, h, d).transpose(0, 2, 3, 1)


def _heads_tiles(a, b, s, h, d, tk):
    return a.reshape(b, s // tk, tk, h, d).transpose(0, 3, 1, 2, 4)


def _heads_t_tiles(a, b, s, h, d, tk):
    return a.reshape(b, s // tk, tk, h, d).transpose(0, 3, 1, 4, 2)


def _from_heads_t(o, b, s):
    return o.transpose(0, 3, 1, 2).reshape(b * s, -1)


def kernel(x, positions, w_in, b_gate, b_forget, mla_q_norm, mla_kv_norm, mla_w_uq, mla_w_ukv, w_branch, w_out,
           ln1_g, ln1_b, ffn_w_up, ffn_conv_w, ffn_conv_b, ffn_w_down, ln2_g, ln2_b):
    bsz, seq, dm = x.shape
    depth = w_in.shape[0]
    d_ff = ffn_w_down.shape[1]
    m = bsz * seq
    tk = ATT_TK
    nk = seq // tk
    topk = min(DSA_MAX_TOPK, seq // 4)
    alpha = (2 * depth) ** 0.25
    slopes = tuple(2.0 ** (-8.0 * i / N_HEADS) for i in range(1, N_HEADS + 1))
    row_tile = min(512, seq)

    cq_tab, sq_tab, ck_tab, sk_tab = _rope_tables(positions)
    pos_row = positions.reshape(bsz, 1, seq)
    pos_col = positions.reshape(bsz, nk, tk, 1)
    ones3 = jnp.ones((bsz, N_HEADS, 3, seq), BF16)
    zeros_pad = jnp.zeros((bsz, N_HEADS, FOX_QK - HEAD_DIM - 6, seq), BF16)

    xf = x.reshape(m, dm)
    xb = xf.astype(BF16)
    for l in range(depth):
        z = _matmul(xb, _in_proj_weight(w_in[l]), F32, tm=min(1024, m), tn=1024)

        wq, wqr, wk, wv, place = _mla_weights(mla_w_uq[l], mla_w_ukv[l])
        q_m, k_m, v_m = _mla_prep(z, cq_tab, sq_tab, ck_tab, sk_tab, mla_q_norm[l].reshape(1, -1),
                                  mla_kv_norm[l].reshape(1, -1), wq, wqr, wk, wv, place, row_tile)
        o_mla = _attention(_heads_t(q_m, bsz, seq, N_HEADS, MLA_QK),
                           _heads_tiles(k_m, bsz, seq, N_HEADS, MLA_QK, tk),
                           _heads_t_tiles(v_m, bsz, seq, N_HEADS, HEAD_DIM, tk))

        zb = z[:, ZB_OFF:ZB_OFF + 1544]
        f_q = (zb[:, 0:512] * (HEAD_DIM ** -0.5 * LOG2E)).astype(BF16)
        f_k = zb[:, 512:1024].astype(BF16)
        f_v = zb[:, 1024:1536].astype(BF16)
        logit_t = zb[:, 1536:1544].reshape(bsz, seq, N_HEADS).transpose(0, 2, 1)
        c_hi, c_mid, c_lo = _fox_cum(logit_t, b_forget[l])
        cum3 = jnp.stack([c_hi, c_mid, c_lo], axis=2)
        fq_t = jnp.concatenate([_heads_t(f_q, bsz, seq, N_HEADS, HEAD_DIM), cum3, ones3, zeros_pad], axis=2)
        fk_t = jnp.concatenate([_heads_t(f_k, bsz, seq, N_HEADS, HEAD_DIM), ones3, -cum3, zeros_pad], axis=2)
        fk = fk_t.reshape(bsz, N_HEADS, FOX_QK, nk, tk).transpose(0, 1, 3, 4, 2)
        o_fox = _attention(fq_t, fk, _heads_t_tiles(f_v, bsz, seq, N_HEADS, HEAD_DIM, tk))

        zc = z[:, ZC_OFF:ZC_OFF + 936]
        s_q = (zc[:, 0:512] * (HEAD_DIM ** -0.5 * LOG2E)).astype(BF16)
        s_k = zc[:, 512:576].astype(BF16).reshape(bsz, nk, tk, HEAD_DIM)
        s_vt = zc[:, 576:640].astype(BF16).reshape(bsz, nk, tk, HEAD_DIM).transpose(0, 1, 3, 2)
        i_q = zc[:, 640:896].astype(BF16)
        i_k = zc[:, 896:928].astype(BF16).reshape(bsz, nk, tk, IDX_DIM)
        i_w = zc[:, 928:936].reshape(bsz, seq, N_HEADS).transpose(0, 2, 1)
        o_dsa = _dsa(_heads_t(s_q, bsz, seq, N_HEADS, HEAD_DIM), s_k, s_vt,
                     _heads_t(i_q, bsz, seq, N_HEADS, IDX_DIM), i_k, i_w, pos_row, pos_col, topk, slopes)

        xf, xb = _merge(_from_heads_t(o_mla, bsz, seq), _from_heads_t(o_fox, bsz, seq), _from_heads_t(o_dsa, bsz, seq),
                        z, xf, w_branch[l].astype(BF16), b_gate[l], w_out[l].astype(BF16),
                        ln1_g[l].reshape(1, -1), ln1_b[l].reshape(1, -1), alpha, min(256, m))

        n_chunks = d_ff // FF_CHUNK
        chunks = lambda w: w.reshape(w.shape[0], n_chunks, FF_CHUNK).transpose(1, 0, 2)
        w_up = ffn_w_up[l]
        x3, xb3 = _ffn(xf.reshape(bsz, seq, dm),
                       chunks(w_up[:, :d_ff]).astype(BF16), chunks(w_up[:, d_ff:]).astype(BF16),
                       chunks(ffn_conv_w[l]), chunks(ffn_conv_b[l].reshape(1, -1)),
                       ffn_w_down[l].reshape(n_chunks, FF_CHUNK, dm).astype(BF16),
                       ln2_g[l].reshape(1, -1), ln2_b[l].reshape(1, -1), alpha, row_tile)
        xf, xb = x3.reshape(m, dm), xb3.reshape(m, dm)
    return xf.reshape(bsz, seq, dm)
```

```python
import functools
import math

import jax
import jax.numpy as jnp
from jax import lax
from jax.experimental import pallas as pl
from jax.experimental.pallas import tpu as pltpu

F32 = jnp.float32
BF16 = jnp.bfloat16

N_HEADS = 8
HEAD_DIM = 64
MLA_Q_RANK = 192
MLA_KV_RANK = 128
MLA_NOPE = 64
MLA_ROPE = 32
MLA_QK = MLA_NOPE + MLA_ROPE
IDX_DIM = 32
N_BRANCH = 3
CONV_WIDTH = 3
ROPE_THETA = 10000.0
DSA_MAX_TOPK = 256
LN_EPS = 1e-5
RMS_EPS = 1e-6
LOG2E = math.log2(math.e)

ZA_W = 384
ZB_OFF, ZB_W = 384, 1664
ZC_OFF, ZC_W = 2048, 1024
ZG_OFF = 3072
FOX_QK = 80

ATT_TQ = 256
ATT_TK = 256
MASKED = -1e30
M_INIT = -5e29
INT_MIN = -2 ** 31
FF_CHUNK = 256
VMEM_LIMIT = 56 * 1024 * 1024


def _cparams(sem):
    return pltpu.CompilerParams(dimension_semantics=sem, vmem_limit_bytes=VMEM_LIMIT)


def _mm_kernel(x_ref, w_ref, o_ref):
    o_ref[...] = jnp.dot(x_ref[...], w_ref[...], preferred_element_type=F32).astype(o_ref.dtype)


def _matmul(x, w, out_dtype, tm, tn):
    m, k = x.shape
    n = w.shape[1]
    return pl.pallas_call(
        _mm_kernel,
        grid=(n // tn, m // tm),
        in_specs=[pl.BlockSpec((tm, k), lambda j, i: (i, 0)),
                  pl.BlockSpec((k, tn), lambda j, i: (0, j))],
        out_specs=pl.BlockSpec((tm, tn), lambda j, i: (i, j)),
        out_shape=jax.ShapeDtypeStruct((m, n), out_dtype),
        compiler_params=_cparams(("parallel", "parallel")),
        name="in_proj",
    )(x, w)


def _mla_prep_kernel(z_ref, cq_ref, sq_ref, ck_ref, sk_ref, qn_ref, kvn_ref, wq_ref, wqr_ref,
                     wk_ref, wv_ref, e_ref, q_out, k_out, v_out):
    z = z_ref[...]
    c_kv = z[:, 0:MLA_KV_RANK]
    c_q = z[:, MLA_KV_RANK:MLA_KV_RANK + MLA_Q_RANK]
    kr = z[:, 320:352]
    kr_rot = z[:, 352:384]
    nq = (c_q * lax.rsqrt(jnp.mean(jnp.square(c_q), -1, keepdims=True) + RMS_EPS) * qn_ref[...]).astype(BF16)
    nkv = (c_kv * lax.rsqrt(jnp.mean(jnp.square(c_kv), -1, keepdims=True) + RMS_EPS) * kvn_ref[...]).astype(BF16)
    q = jnp.dot(nq, wq_ref[...], preferred_element_type=F32)
    q_rot = jnp.dot(nq, wqr_ref[...], preferred_element_type=F32)
    q_out[...] = (q * cq_ref[...] + q_rot * sq_ref[...]).astype(BF16)
    k_rope = (kr * ck_ref[...] + kr_rot * sk_ref[...]).astype(BF16)
    k = jnp.dot(nkv, wk_ref[...], preferred_element_type=F32) + jnp.dot(k_rope, e_ref[...], preferred_element_type=F32)
    k_out[...] = k.astype(BF16)
    v_out[...] = jnp.dot(nkv, wv_ref[...], preferred_element_type=F32).astype(BF16)


def _mla_prep(z, cq_tab, sq_tab, ck_tab, sk_tab, qn, kvn, wq, wqr, wk, wv, e, tm):
    m = z.shape[0]
    row = lambda w: pl.BlockSpec((tm, w), lambda i: (i, 0))
    full = lambda a: pl.BlockSpec(a.shape, lambda i: (0,) * a.ndim)
    return pl.pallas_call(
        _mla_prep_kernel,
        grid=(m // tm,),
        in_specs=[row(ZA_W), row(N_HEADS * MLA_QK), row(N_HEADS * MLA_QK), row(MLA_ROPE), row(MLA_ROPE),
                  full(qn), full(kvn), full(wq), full(wqr), full(wk), full(wv), full(e)],
        out_specs=[row(N_HEADS * MLA_QK), row(N_HEADS * MLA_QK), row(N_HEADS * HEAD_DIM)],
        out_shape=[jax.ShapeDtypeStruct((m, N_HEADS * MLA_QK), BF16),
                   jax.ShapeDtypeStruct((m, N_HEADS * MLA_QK), BF16),
                   jax.ShapeDtypeStruct((m, N_HEADS * HEAD_DIM), BF16)],
        compiler_params=_cparams(("parallel",)),
        name="mla_prep",
    )(z, cq_tab, sq_tab, ck_tab, sk_tab, qn, kvn, wq, wqr, wk, wv, e)


def _fox_cum_kernel(x_ref, b_ref, tri_ref, hi_ref, mid_ref, lo_ref, *, chunk):
    seq = x_ref.shape[2]
    lf = jax.nn.log_sigmoid(x_ref[0] + b_ref[...])
    carry = jnp.zeros((N_HEADS, 1), F32)
    for c in range(seq // chunk):
        seg = lf[:, c * chunk:(c + 1) * chunk]
        cs = jnp.dot(seg, tri_ref[...], preferred_element_type=F32, precision=lax.Precision.HIGHEST) + carry
        carry = cs[:, chunk - 1:chunk]
        c2 = cs * LOG2E
        hi = c2.astype(BF16)
        r1 = c2 - hi.astype(F32)
        mid = r1.astype(BF16)
        lo = (r1 - mid.astype(F32)).astype(BF16)
        hi_ref[0, :, c * chunk:(c + 1) * chunk] = hi.astype(F32)
        mid_ref[0, :, c * chunk:(c + 1) * chunk] = mid.astype(F32)
        lo_ref[0, :, c * chunk:(c + 1) * chunk] = lo.astype(F32)


def _fox_cum(logit_t, b_forget):
    b, h, s = logit_t.shape
    chunk = 256
    tri = (jnp.arange(chunk)[:, None] <= jnp.arange(chunk)[None, :]).astype(F32)
    spec = pl.BlockSpec((1, h, s), lambda i: (i, 0, 0))
    return pl.pallas_call(
        functools.partial(_fox_cum_kernel, chunk=chunk),
        grid=(b,),
        in_specs=[spec, pl.BlockSpec((h, 1), lambda i: (0, 0)), pl.BlockSpec((chunk, chunk), lambda i: (0, 0))],
        out_specs=[spec, spec, spec],
        out_shape=[jax.ShapeDtypeStruct((b, h, s), F32)] * 3,
        compiler_params=_cparams(("parallel",)),
        name="fox_cum",
    )(logit_t, b_forget.reshape(h, 1), tri)


def _softmax_step(s, v_t, m_ref, l_ref, acc_ref, h):
    m_old = m_ref[h]
    m_new = jnp.maximum(m_old, jnp.max(s, axis=0, keepdims=True))
    p = jnp.exp2(s - m_new)
    alpha = jnp.exp2(m_old - m_new)
    l_ref[h] = alpha * l_ref[h] + jnp.sum(p, axis=0, keepdims=True)
    acc_ref[h] = alpha * acc_ref[h] + jnp.dot(v_t, p.astype(BF16), preferred_element_type=F32)
    m_ref[h] = m_new


def _causal_mask(tk, tq):
    return lax.broadcasted_iota(jnp.int32, (tk, tq), 0) <= lax.broadcasted_iota(jnp.int32, (tk, tq), 1)


def _pipelined_tiles(j, score, consume):
    score(0, 0)

    def pair(p, carry):
        t = 2 * p
        score(t + 1, 1)
        consume(t, 0, False)
        score(t + 2, 0)
        consume(t + 1, 1, False)
        return carry

    lax.fori_loop(0, j // 2, pair, 0)
    odd = j % 2 == 1

    @pl.when(odd)
    def _():
        score(j, 1)
        consume(j - 1, 0, False)
        consume(j, 1, True)

    @pl.when(jnp.logical_not(odd))
    def _():
        consume(j, 0, True)


def _attn_kernel(qt_ref, k_ref, vt_ref, o_ref, s_ref, m_ref, l_ref, acc_ref):
    j = pl.program_id(1)
    tk, tq = k_ref.shape[3], qt_ref.shape[3]
    m_ref[...] = jnp.full(m_ref.shape, M_INIT, F32)
    l_ref[...] = jnp.zeros(l_ref.shape, F32)
    acc_ref[...] = jnp.zeros(acc_ref.shape, F32)

    def score(i, slot):
        for h in range(N_HEADS):
            s_ref[slot, h] = jnp.dot(k_ref[0, h, i], qt_ref[0, h], preferred_element_type=F32)

    def consume(i, slot, diag):
        for h in range(N_HEADS):
            s = s_ref[slot, h]
            if diag:
                s = jnp.where(_causal_mask(tk, tq), s, MASKED)
            _softmax_step(s, vt_ref[0, h, i], m_ref, l_ref, acc_ref, h)

    _pipelined_tiles(j, score, consume)
    for h in range(N_HEADS):
        o_ref[0, h] = (acc_ref[h] / l_ref[h]).astype(o_ref.dtype)


def _attention(qt, k, vt):
    b, h, dk, s = qt.shape
    nk, tk = k.shape[2], k.shape[3]
    dv = vt.shape[3]
    tq = ATT_TQ
    return pl.pallas_call(
        _attn_kernel,
        grid=(b, s // tq),
        in_specs=[pl.BlockSpec((1, h, dk, tq), lambda bi, j: (bi, 0, 0, j)),
                  pl.BlockSpec((1, h, nk, tk, dk), lambda bi, j: (bi, 0, 0, 0, 0)),
                  pl.BlockSpec((1, h, nk, dv, tk), lambda bi, j: (bi, 0, 0, 0, 0))],
        out_specs=pl.BlockSpec((1, h, dv, tq), lambda bi, j: (bi, 0, 0, j)),
        out_shape=jax.ShapeDtypeStruct((b, h, dv, s), BF16),
        scratch_shapes=[pltpu.VMEM((2, h, tk, tq), F32),
                        pltpu.VMEM((h, 1, tq), F32), pltpu.VMEM((h, 1, tq), F32), pltpu.VMEM((h, dv, tq), F32)],
        compiler_params=_cparams(("parallel", "arbitrary")),
        name="flash_attn",
    )(qt, k, vt)


def _sortable_key(x):
    bits = lax.bitcast_convert_type(x, jnp.int32)
    return bits ^ ((bits >> 31) & 0x7FFFFFFF)


def _count_ge(keys_ref, n_tiles, cand, pred=None):
    tk, tq = keys_ref.shape[1], keys_ref.shape[2]

    def body(i, c):
        hit = keys_ref[i] >= cand
        return c + jnp.sum(hit.astype(jnp.int32).reshape(tk // 8, 8, tq), axis=0)

    c = lax.fori_loop(0, n_tiles, body, jnp.zeros((8, tq), jnp.int32))
    return jnp.sum(c, axis=0, keepdims=True)


def _count16(ref, n_tiles, cand, strict=False):
    tk, tq = ref.shape[1], ref.shape[2]
    c16 = cand.astype(jnp.int16)

    def body(i, c):
        k = ref[i]
        hit = (k > c16) if strict else (k >= c16)
        one = jnp.where(hit, jnp.int16(1), jnp.int16(0))
        for r in range(tk // 16):
            c = c + one[16 * r:16 * (r + 1), :]
        return c

    c = lax.fori_loop(0, n_tiles, body, jnp.zeros((16, tq), jnp.int16))
    return jnp.sum(c.astype(jnp.int32), axis=0, keepdims=True)


def _kth_largest16(ref, n_tiles, kth):
    tq = ref.shape[2]
    cnt = _count16(ref, n_tiles, jnp.zeros((1, tq), jnp.int32))
    prefix = jnp.where(cnt >= kth, jnp.int32(0), jnp.int32(-32768))

    def bit_body(it, prefix):
        cand = prefix + jnp.left_shift(jnp.int32(1), 14 - it)
        return jnp.where(_count16(ref, n_tiles, cand) >= kth, cand, prefix)

    return lax.fori_loop(0, 15, bit_body, prefix)


def _dsa_kernel(sqt_ref, sk_ref, svt_ref, iqt_ref, ik_ref, iw_ref, prow_ref, pcol_ref, o_ref,
                keys_ref, khi_ref, klo_ref, s_ref, m_ref, l_ref, acc_ref, *, topk, slopes):
    j = pl.program_id(1)
    tk, tq = sk_ref.shape[2], sqt_ref.shape[3]
    n_tiles = j + 1
    causal = _causal_mask(tk, tq)

    def score_tile(i, diag):
        ik = ik_ref[0, i]
        sc = jnp.zeros((tk, tq), F32)
        for h in range(N_HEADS):
            lg = jnp.dot(ik, iqt_ref[0, h], preferred_element_type=F32)
            sc = sc + iw_ref[0, h:h + 1, :] * jnp.maximum(lg, 0.0)
        key = _sortable_key(sc)
        if diag:
            key = jnp.where(causal, key, INT_MIN)
        keys_ref[i] = key
        khi_ref[i] = (key >> 16).astype(jnp.int16)
        klo_ref[i] = ((key & 0xFFFF) - 32768).astype(jnp.int16)

    def score_body(i, carry):
        score_tile(i, False)
        return carry

    lax.fori_loop(0, j, score_body, 0)
    score_tile(j, True)

    t_hi = _kth_largest16(khi_ref, n_tiles, topk)
    need_lo = topk - _count16(khi_ref, n_tiles, t_hi, strict=True)
    t_hi16 = t_hi.astype(jnp.int16)

    def class_body(i, carry):
        klo_ref[i] = jnp.where(khi_ref[i] == t_hi16, klo_ref[i], jnp.int16(-32768))
        return carry

    lax.fori_loop(0, n_tiles, class_body, 0)
    t_lo = _kth_largest16(klo_ref, n_tiles, need_lo)
    thr = t_hi * 65536 + (t_lo + 32768)

    real = thr > INT_MIN
    cnt_ge = _count_ge(keys_ref, n_tiles, thr)
    tie = jnp.logical_and(real, cnt_ge > topk)
    any_tie = jnp.max(tie.astype(jnp.int32)) > 0

    @pl.when(any_tie)
    def _():
        def count_eq_below(limit):
            def body(i, c):
                idx = i * tk + lax.broadcasted_iota(jnp.int32, (tk, tq), 0)
                hit = jnp.logical_and(keys_ref[i] == thr, idx < limit)
                return c + jnp.sum(hit.astype(jnp.int32).reshape(tk // 8, 8, tq), axis=0)
            c = lax.fori_loop(0, n_tiles, body, jnp.zeros((8, tq), jnp.int32))
            return jnp.sum(c, axis=0, keepdims=True)

        cnt_gt = cnt_ge - count_eq_below(jnp.full((1, tq), n_tiles * tk, jnp.int32))
        need = topk - cnt_gt
        n_bits = max(1, int(math.ceil(math.log2(keys_ref.shape[0] * tk))))

        def idx_body(it, lo):
            t = lo + jnp.left_shift(jnp.int32(1), n_bits - 1 - it)
            return jnp.where(count_eq_below(t) < need, t, lo)

        last = lax.fori_loop(0, n_bits, idx_body, jnp.zeros((1, tq), jnp.int32))

        def drop_body(i, carry):
            idx = i * tk + lax.broadcasted_iota(jnp.int32, (tk, tq), 0)
            k = keys_ref[i]
            drop = jnp.logical_and(jnp.logical_and(k == thr, idx > last), tie)
            keys_ref[i] = jnp.where(drop, INT_MIN, k)
            return carry

        lax.fori_loop(0, n_tiles, drop_body, 0)

    thr_eff = jnp.maximum(thr, INT_MIN + 1)

    m_ref[...] = jnp.full(m_ref.shape, M_INIT, F32)
    l_ref[...] = jnp.zeros(l_ref.shape, F32)
    acc_ref[...] = jnp.zeros(acc_ref.shape, F32)
    p_row = prow_ref[0]

    def score(i, slot):
        k_t = sk_ref[0, i]
        for h in range(N_HEADS):
            s_ref[slot, h] = jnp.dot(k_t, sqt_ref[0, h], preferred_element_type=F32)

    def consume(i, slot, diag):
        del diag
        sel = keys_ref[i] >= thr_eff
        dist = jnp.abs(p_row - pcol_ref[0, i]).astype(F32)
        v_t = svt_ref[0, i]
        for h in range(N_HEADS):
            s = s_ref[slot, h] - (slopes[h] * LOG2E) * dist
            _softmax_step(jnp.where(sel, s, MASKED), v_t, m_ref, l_ref, acc_ref, h)

    _pipelined_tiles(j, score, consume)
    for h in range(N_HEADS):
        o_ref[0, h] = (acc_ref[h] / l_ref[h]).astype(o_ref.dtype)


def _dsa(sqt, sk, svt, iqt, ik, iw_t, pos_row, pos_col, topk, slopes):
    b, h, d, s = sqt.shape
    nk, tk = sk.shape[1], sk.shape[2]
    tq = ATT_TQ
    kv = lambda a: pl.BlockSpec((1,) + a.shape[1:], lambda bi, j: (bi,) + (0,) * (a.ndim - 1))
    return pl.pallas_call(
        functools.partial(_dsa_kernel, topk=topk, slopes=slopes),
        grid=(b, s // tq),
        in_specs=[pl.BlockSpec((1, h, d, tq), lambda bi, j: (bi, 0, 0, j)),
                  kv(sk), kv(svt),
                  pl.BlockSpec((1, h, IDX_DIM, tq), lambda bi, j: (bi, 0, 0, j)),
                  kv(ik),
                  pl.BlockSpec((1, h, tq), lambda bi, j: (bi, 0, j)),
                  pl.BlockSpec((1, 1, tq), lambda bi, j: (bi, 0, j)),
                  kv(pos_col)],
        out_specs=pl.BlockSpec((1, h, d, tq), lambda bi, j: (bi, 0, 0, j)),
        out_shape=jax.ShapeDtypeStruct((b, h, d, s), BF16),
        scratch_shapes=[pltpu.VMEM((nk, tk, tq), jnp.int32),
                        pltpu.VMEM((nk, tk, tq), jnp.int16), pltpu.VMEM((nk, tk, tq), jnp.int16),
                        pltpu.VMEM((2, h, tk, tq), F32),
                        pltpu.VMEM((h, 1, tq), F32), pltpu.VMEM((h, 1, tq), F32), pltpu.VMEM((h, d, tq), F32)],
        compiler_params=_cparams(("parallel", "arbitrary")),
        name="dsa",
    )(sqt, sk, svt, iqt, ik, iw_t, pos_row, pos_col)


def _layernorm(r, g, b):
    mu = jnp.mean(r, -1, keepdims=True)
    d = r - mu
    var = jnp.mean(jnp.square(d), -1, keepdims=True)
    return d * lax.rsqrt(var + LN_EPS) * g + b


def _merge_kernel(o1_ref, o2_ref, o3_ref, g_ref, x_ref, wb_ref, bg_ref, wo_ref, lng_ref, lnb_ref,
                  xo_ref, xb_ref, *, alpha):
    dm = x_ref.shape[1]
    merged = jnp.zeros(x_ref.shape, F32)
    for n, o_ref in enumerate((o1_ref, o2_ref, o3_ref)):
        proj = jnp.dot(o_ref[...], wb_ref[n], preferred_element_type=F32)
        gate = jax.nn.sigmoid(g_ref[:, n * dm:(n + 1) * dm] + bg_ref[n:n + 1, :])
        merged = merged + gate * proj
    y = jnp.dot(merged.astype(BF16), wo_ref[...], preferred_element_type=F32)
    out = _layernorm(alpha * x_ref[...] + y, lng_ref[...], lnb_ref[...])
    xo_ref[...] = out
    xb_ref[...] = out.astype(BF16)


def _merge(o1, o2, o3, z, x, wb, bg, wo, lng, lnb, alpha, tm):
    m, dm = x.shape
    bw = o1.shape[1]
    row = lambda w: pl.BlockSpec((tm, w), lambda i: (i, 0))
    full = lambda a: pl.BlockSpec(a.shape, lambda i: (0,) * a.ndim)
    return pl.pallas_call(
        functools.partial(_merge_kernel, alpha=alpha),
        grid=(m // tm,),
        in_specs=[row(bw), row(bw), row(bw),
                  pl.BlockSpec((tm, N_BRANCH * dm), lambda i: (i, ZG_OFF // (N_BRANCH * dm))),
                  row(dm), full(wb), full(bg), full(wo), full(lng), full(lnb)],
        out_specs=[row(dm), row(dm)],
        out_shape=[jax.ShapeDtypeStruct((m, dm), F32), jax.ShapeDtypeStruct((m, dm), BF16)],
        compiler_params=_cparams(("parallel",)),
        name="merge",
    )(o1, o2, o3, z, x, wb, bg, wo, lng, lnb)


def _ffn_kernel(x_ref, wg_ref, wv_ref, cw_ref, cb_ref, wd_ref, lng_ref, lnb_ref, xo_ref, xb_ref,
                tail_ref, y_ref, *, alpha):
    tm = x_ref.shape[1]
    n_chunks = wg_ref.shape[0]
    first = pl.program_id(1) == 0
    x = x_ref[0]
    xb = x.astype(BF16)
    row = lax.broadcasted_iota(jnp.int32, (tm, FF_CHUNK), 0)
    y_ref[...] = jnp.zeros(y_ref.shape, F32)

    @pl.when(first)
    def _():
        tail_ref[...] = jnp.zeros(tail_ref.shape, F32)

    def body(c, carry):
        g = jnp.dot(xb, wg_ref[c], preferred_element_type=F32)
        v = jnp.dot(xb, wv_ref[c], preferred_element_type=F32)
        tail = tail_ref[c]
        g1 = jnp.where(row < 1, tail[7:8, :], pltpu.roll(g, 1, axis=0))
        g2 = jnp.where(row < 2, jnp.where(row < 1, tail[6:7, :], tail[7:8, :]), pltpu.roll(g, 2, axis=0))
        cw = cw_ref[c]
        conv = cb_ref[c] + g2 * cw[0:1, :] + g1 * cw[1:2, :] + g * cw[2:3, :]
        gelu = 0.5 * conv * (1.0 + lax.erf(conv * (2.0 ** -0.5)))
        hid = (gelu * v).astype(BF16)
        y_ref[...] += jnp.dot(hid, wd_ref[c], preferred_element_type=F32)
        tail_ref[c] = g[tm - 8:tm, :]
        return carry

    lax.fori_loop(0, n_chunks, body, 0)
    out = _layernorm(alpha * x + y_ref[...], lng_ref[...], lnb_ref[...])
    xo_ref[0] = out
    xb_ref[0] = out.astype(BF16)


def _ffn(x, wg, wv, cw, cb, wd, lng, lnb, alpha, tm):
    b, s, dm = x.shape
    n_chunks = wg.shape[0]
    full = lambda a: pl.BlockSpec(a.shape, lambda bi, i: (0,) * a.ndim, pipeline_mode=pl.Buffered(1))
    blk = pl.BlockSpec((1, tm, dm), lambda bi, i: (bi, i, 0))
    return pl.pallas_call(
        functools.partial(_ffn_kernel, alpha=alpha),
        grid=(b, s // tm),
        in_specs=[blk, full(wg), full(wv), full(cw), full(cb), full(wd), full(lng), full(lnb)],
        out_specs=[blk, blk],
        out_shape=[jax.ShapeDtypeStruct((b, s, dm), F32), jax.ShapeDtypeStruct((b, s, dm), BF16)],
        scratch_shapes=[pltpu.VMEM((n_chunks, 8, FF_CHUNK), F32), pltpu.VMEM((tm, dm), F32)],
        compiler_params=_cparams(("parallel", "arbitrary")),
        name="ffn",
    )(x, wg, wv, cw, cb, wd, lng, lnb)


def _split_offsets(d_model):
    sizes = (MLA_Q_RANK, MLA_KV_RANK, MLA_ROPE, 512, 512, 512, N_HEADS, 512, HEAD_DIM, HEAD_DIM,
             N_HEADS * IDX_DIM, IDX_DIM, N_HEADS, N_BRANCH * d_model)
    offs, o = [], 0
    for w in sizes:
        offs.append((o, o + w))
        o += w
    return offs


def _in_proj_weight(w_in):
    d = w_in.shape[0]
    (c_q, c_kv, k_r, f_q, f_k, f_v, f_l, s_q, s_k, s_v, i_q, i_k, i_w, gate) = [
        w_in[:, a:b] for a, b in _split_offsets(d)]
    half = MLA_ROPE // 2
    k_r_rot = jnp.concatenate([-k_r[:, half:], k_r[:, :half]], axis=1)
    pad = lambda n: jnp.zeros((d, n), w_in.dtype)
    za = jnp.concatenate([c_kv, c_q, k_r, k_r_rot], axis=1)
    zb = jnp.concatenate([f_q, f_k, f_v, f_l, pad(ZB_W - 1544)], axis=1)
    zc = jnp.concatenate([s_q, s_k, s_v, i_q, i_k, i_w, pad(ZC_W - 936)], axis=1)
    return jnp.concatenate([za, zb, zc, gate], axis=1).astype(BF16)


def _mla_weights(w_uq, w_ukv):
    r = w_uq.shape[0]
    half = MLA_ROPE // 2
    wq = w_uq.reshape(r, N_HEADS, MLA_QK)
    rope = wq[:, :, MLA_NOPE:]
    wq_rot = jnp.concatenate([jnp.zeros((r, N_HEADS, MLA_NOPE), w_uq.dtype), -rope[:, :, half:], rope[:, :, :half]], axis=2)
    wkv = w_ukv.reshape(w_ukv.shape[0], N_HEADS, MLA_NOPE + HEAD_DIM)
    wk = jnp.concatenate([wkv[:, :, :MLA_NOPE], jnp.zeros((wkv.shape[0], N_HEADS, MLA_ROPE), w_ukv.dtype)], axis=2)
    wv = wkv[:, :, MLA_NOPE:]
    place = jnp.zeros((MLA_ROPE, N_HEADS, MLA_QK), F32)
    place = place.at[jnp.arange(MLA_ROPE), :, MLA_NOPE + jnp.arange(MLA_ROPE)].set(1.0)
    flat = lambda a: a.reshape(a.shape[0], -1).astype(BF16)
    return flat(w_uq), flat(wq_rot), flat(wk), flat(wv), flat(place)


def _rope_tables(positions):
    half = MLA_ROPE // 2
    inv = ROPE_THETA ** (-jnp.arange(half, dtype=F32) / half)
    ang = positions.astype(F32)[..., None] * inv
    cos, sin = jnp.cos(ang), jnp.sin(ang)
    b, s = positions.shape
    q_scale = (MLA_QK ** -0.5) * LOG2E
    cos2 = jnp.concatenate([cos, cos], -1)
    sin2 = jnp.concatenate([sin, sin], -1)
    ones = jnp.ones((b, s, MLA_NOPE), F32)
    cq = jnp.tile(jnp.concatenate([ones, cos2], -1) * q_scale, (1, 1, N_HEADS))
    sq = jnp.tile(jnp.concatenate([0.0 * ones, sin2], -1) * q_scale, (1, 1, N_HEADS))
    flat = lambda a: a.reshape(b * s, a.shape[-1])
    return flat(cq), flat(sq), flat(cos2), flat(sin2)


def _heads_t(a, b, s, h, d):
    return a.reshape(b, s, h, d).transpose(0, 2, 3, 1)


def _heads_tiles(a, b, s, h, d, tk):
    return a.reshape(b, s // tk, tk, h, d).transpose(0, 3, 1, 2, 4)


def _heads_t_tiles(a, b, s, h, d, tk):
    return a.reshape(b, s // tk, tk, h, d).transpose(0, 3, 1, 4, 2)


def _from_heads_t(o, b, s):
    return o.transpose(0, 3, 1, 2).reshape(b * s, -1)


def kernel(x, positions, w_in, b_gate, b_forget, mla_q_norm, mla_kv_norm, mla_w_uq, mla_w_ukv, w_branch, w_out,
           ln1_g, ln1_b, ffn_w_up, ffn_conv_w, ffn_conv_b, ffn_w_down, ln2_g, ln2_b):
    bsz, seq, dm = x.shape
    depth = w_in.shape[0]
    d_ff = ffn_w_down.shape[1]
    m = bsz * seq
    tk = ATT_TK
    nk = seq // tk
    topk = min(DSA_MAX_TOPK, seq // 4)
    alpha = (2 * depth) ** 0.25
    slopes = tuple(2.0 ** (-8.0 * i / N_HEADS) for i in range(1, N_HEADS + 1))
    row_tile = min(512, seq)

    cq_tab, sq_tab, ck_tab, sk_tab = _rope_tables(positions)
    pos_row = positions.reshape(bsz, 1, seq)
    pos_col = positions.reshape(bsz, nk, tk, 1)
    ones3 = jnp.ones((bsz, N_HEADS, 3, seq), BF16)
    zeros_pad = jnp.zeros((bsz, N_HEADS, FOX_QK - HEAD_DIM - 6, seq), BF16)

    xf = x.reshape(m, dm)
    xb = xf.astype(BF16)
    for l in range(depth):
        z = _matmul(xb, _in_proj_weight(w_in[l]), F32, tm=min(1024, m), tn=1024)

        wq, wqr, wk, wv, place = _mla_weights(mla_w_uq[l], mla_w_ukv[l])
        q_m, k_m, v_m = _mla_prep(z, cq_tab, sq_tab, ck_tab, sk_tab, mla_q_norm[l].reshape(1, -1),
                                  mla_kv_norm[l].reshape(1, -1), wq, wqr, wk, wv, place, row_tile)
        o_mla = _attention(_heads_t(q_m, bsz, seq, N_HEADS, MLA_QK),
                           _heads_tiles(k_m, bsz, seq, N_HEADS, MLA_QK, tk),
                           _heads_t_tiles(v_m, bsz, seq, N_HEADS, HEAD_DIM, tk))

        zb = z[:, ZB_OFF:ZB_OFF + 1544]
        f_q = (zb[:, 0:512] * (HEAD_DIM ** -0.5 * LOG2E)).astype(BF16)
        f_k = zb[:, 512:1024].astype(BF16)
        f_v = zb[:, 1024:1536].astype(BF16)
        logit_t = zb[:, 1536:1544].reshape(bsz, seq, N_HEADS).transpose(0, 2, 1)
        c_hi, c_mid, c_lo = _fox_cum(logit_t, b_forget[l])
        cum3 = jnp.stack([c_hi, c_mid, c_lo], axis=2)
        fq_t = jnp.concatenate([_heads_t(f_q, bsz, seq, N_HEADS, HEAD_DIM), cum3, ones3, zeros_pad], axis=2)
        fk_t = jnp.concatenate([_heads_t(f_k, bsz, seq, N_HEADS, HEAD_DIM), ones3, -cum3, zeros_pad], axis=2)
        fk = fk_t.reshape(bsz, N_HEADS, FOX_QK, nk, tk).transpose(0, 1, 3, 4, 2)
        o_fox = _attention(fq_t, fk, _heads_t_tiles(f_v, bsz, seq, N_HEADS, HEAD_DIM, tk))

        zc = z[:, ZC_OFF:ZC_OFF + 936]
        s_q = (zc[:, 0:512] * (HEAD_DIM ** -0.5 * LOG2E)).astype(BF16)
        s_k = zc[:, 512:576].astype(BF16).reshape(bsz, nk, tk, HEAD_DIM)
        s_vt = zc[:, 576:640].astype(BF16).reshape(bsz, nk, tk, HEAD_DIM).transpose(0, 1, 3, 2)
        i_q = zc[:, 640:896].astype(BF16)
        i_k = zc[:, 896:928].astype(BF16).reshape(bsz, nk, tk, IDX_DIM)
        i_w = zc[:, 928:936].reshape(bsz, seq, N_HEADS).transpose(0, 2, 1)
        o_dsa = _dsa(_heads_t(s_q, bsz, seq, N_HEADS, HEAD_DIM), s_k, s_vt,
                     _heads_t(i_q, bsz, seq, N_HEADS, IDX_DIM), i_k, i_w, pos_row, pos_col, topk, slopes)

        xf, xb = _merge(_from_heads_t(o_mla, bsz, seq), _from_heads_t(o_fox, bsz, seq), _from_heads_t(o_dsa, bsz, seq),
                        z, xf, w_branch[l].astype(BF16), b_gate[l], w_out[l].astype(BF16),
                        ln1_g[l].reshape(1, -1), ln1_b[l].reshape(1, -1), alpha, min(256, m))

        n_chunks = d_ff // FF_CHUNK
        chunks = lambda w: w.reshape(w.shape[0], n_chunks, FF_CHUNK).transpose(1, 0, 2)
        w_up = ffn_w_up[l]
        x3, xb3 = _ffn(xf.reshape(bsz, seq, dm),
                       chunks(w_up[:, :d_ff]).astype(BF16), chunks(w_up[:, d_ff:]).astype(BF16),
                       chunks(ffn_conv_w[l]), chunks(ffn_conv_b[l].reshape(1, -1)),
                       ffn_w_down[l].reshape(n_chunks, FF_CHUNK, dm).astype(BF16),
                       ln2_g[l].reshape(1, -1), ln2_b[l].reshape(1, -1), alpha, row_tile)
        xf, xb = x3.reshape(m, dm), xb3.reshape(m, dm)
    return xf.reshape(bsz, seq, dm)
```

```python
import functools
import math

import jax
import jax.numpy as jnp
from jax import lax
from jax.experimental import pallas as pl
from jax.experimental.pallas import tpu as pltpu

F32 = jnp.float32
BF16 = jnp.bfloat16

N_HEADS = 8
HEAD_DIM = 64
MLA_Q_RANK = 192
MLA_KV_RANK = 128
MLA_NOPE = 64
MLA_ROPE = 32
MLA_QK = MLA_NOPE + MLA_ROPE
IDX_DIM = 32
N_BRANCH = 3
CONV_WIDTH = 3
ROPE_THETA = 10000.0
DSA_MAX_TOPK = 256
LN_EPS = 1e-5
RMS_EPS = 1e-6
LOG2E = math.log2(math.e)
LANES = 128

ZR_CKV, ZR_CQ, ZR_KR, ZR_KRR = 0, 128, 320, 352
ZR_FL = 384
ZR_FK = 512
ZR_SK = 1536
ZR_IK = 1664
ZR_W = 1792
ZC_FQ, ZC_FV, ZC_SQ, ZC_SV, ZC_IQ, ZC_IW, ZC_FL, ZC_G = 0, 512, 1024, 1536, 1600, 1856, 1872, 1888
FOX_QK = 80
FOX_PAD = 16

ATT_TQ = 256
ATT_TK = 256
MASKED = -1e30
M_INIT = -5e29
INT_MIN = -2 ** 31
FF_CHUNK = 256
VMEM_LIMIT = 56 * 1024 * 1024


def _cparams(sem):
    return pltpu.CompilerParams(dimension_semantics=sem, vmem_limit_bytes=VMEM_LIMIT)


def _resident(a):
    return pl.BlockSpec(a.shape, lambda *_: (0,) * a.ndim, pipeline_mode=pl.Buffered(1))


def _nt(w, x):
    return lax.dot_general(w, x, (((1,), (1,)), ((), ())), preferred_element_type=F32)


def _split3(c):
    hi = c.astype(BF16)
    r1 = c - hi.astype(F32)
    mid = r1.astype(BF16)
    lo = (r1 - mid.astype(F32)).astype(BF16)
    return hi, mid, lo


def _proj_kernel(x_ref, cq_ref, sq_ref, ck_ref, sk_ref, wr_ref, wc_ref, qn_ref, kvn_ref, wq_ref, wqr_ref,
                 wk_ref, wv_ref, e_ref, bfr_ref, bfc_ref, tril_ref, triu_ref, pk_ref, onek_ref, pq_ref, oneq_ref,
                 bg_ref, mq_o, mk_o, mv_o, fq_o, fk_o, fv_o, sq_o, sk_o, sv_o, iq_o, ik_o, iw_o, g_o,
                 cr_ref, cc_ref):
    tm = x_ref.shape[1]

    @pl.when(pl.program_id(1) == 0)
    def _():
        cr_ref[...] = jnp.zeros(cr_ref.shape, F32)
        cc_ref[...] = jnp.zeros(cc_ref.shape, F32)

    x = x_ref[0]
    zr = jnp.dot(x, wr_ref[...], preferred_element_type=F32)
    zc = _nt(wc_ref[0:ZC_G, :], x)

    c_kv = zr[:, ZR_CKV:ZR_CKV + MLA_KV_RANK]
    c_q = zr[:, ZR_CQ:ZR_CQ + MLA_Q_RANK]
    nq = (c_q * lax.rsqrt(jnp.mean(jnp.square(c_q), -1, keepdims=True) + RMS_EPS) * qn_ref[...]).astype(BF16)
    nkv = (c_kv * lax.rsqrt(jnp.mean(jnp.square(c_kv), -1, keepdims=True) + RMS_EPS) * kvn_ref[...]).astype(BF16)
    q_m = _nt(wq_ref[...], nq) * cq_ref[0] + _nt(wqr_ref[...], nq) * sq_ref[0]
    k_rope = (zr[:, ZR_KR:ZR_KR + MLA_ROPE] * ck_ref[0] + zr[:, ZR_KRR:ZR_KRR + MLA_ROPE] * sk_ref[0]).astype(BF16)
    k_m = (jnp.dot(nkv, wk_ref[...], preferred_element_type=F32)
           + jnp.dot(k_rope, e_ref[...], preferred_element_type=F32))
    v_m = _nt(wv_ref[...], nkv)

    lf_r = jax.nn.log_sigmoid(zr[:, ZR_FL:ZR_FL + FOX_PAD] + bfr_ref[...])
    cum_r = jnp.dot(tril_ref[...], lf_r, preferred_element_type=F32, precision=lax.Precision.HIGHEST) + cr_ref[...]
    lf_c = jax.nn.log_sigmoid(zc[ZC_FL:ZC_FL + FOX_PAD] + bfc_ref[...])
    cum_c = jnp.dot(lf_c, triu_ref[...], preferred_element_type=F32, precision=lax.Precision.HIGHEST) + cc_ref[...]
    cr_ref[...] = cum_r[tm - 1:tm, :]
    cc_ref[...] = cum_c[:, tm - 1:tm]
    k_bias = onek_ref[...]
    for n, term in enumerate(_split3(cum_r * LOG2E)):
        k_bias = k_bias + jnp.dot(term, pk_ref[n], preferred_element_type=F32)
    q_bias = oneq_ref[...]
    for n, term in enumerate(_split3(cum_c * LOG2E)):
        q_bias = q_bias + jnp.dot(pq_ref[n], term, preferred_element_type=F32)
    f_k = zr[:, ZR_FK:ZR_FK + N_HEADS * LANES] + k_bias
    q_scale = HEAD_DIM ** -0.5 * LOG2E
    f_q = zc[ZC_FQ:ZC_FQ + N_HEADS * HEAD_DIM] * q_scale
    s_q = zc[ZC_SQ:ZC_SQ + N_HEADS * HEAD_DIM] * q_scale

    for h in range(N_HEADS):
        mq_o[0, h] = q_m[MLA_QK * h:MLA_QK * (h + 1)].astype(BF16)
        mk_o[0, h, 0] = k_m[:, LANES * h:LANES * h + MLA_QK].astype(BF16)
        mv_o[0, h, 0] = v_m[HEAD_DIM * h:HEAD_DIM * (h + 1)].astype(BF16)
        fq_o[0, h, 0:HEAD_DIM, :] = f_q[HEAD_DIM * h:HEAD_DIM * (h + 1)].astype(BF16)
        fq_o[0, h, HEAD_DIM:FOX_QK, :] = q_bias[FOX_PAD * h:FOX_PAD * (h + 1)].astype(BF16)
        fk_o[0, h, 0] = f_k[:, LANES * h:LANES * h + FOX_QK].astype(BF16)
        fv_o[0, h, 0] = zc[ZC_FV + HEAD_DIM * h:ZC_FV + HEAD_DIM * (h + 1)].astype(BF16)
        sq_o[0, h] = s_q[HEAD_DIM * h:HEAD_DIM * (h + 1)].astype(BF16)
        iq_o[0, h] = zc[ZC_IQ + IDX_DIM * h:ZC_IQ + IDX_DIM * (h + 1)].astype(BF16)

    sk_o[0, 0] = zr[:, ZR_SK:ZR_SK + HEAD_DIM].astype(BF16)
    sv_o[0, 0] = zc[ZC_SV:ZC_SV + HEAD_DIM].astype(BF16)
    ik_o[0, 0] = zr[:, ZR_IK:ZR_IK + IDX_DIM].astype(BF16)
    iw_o[0] = zc[ZC_IW:ZC_IW + N_HEADS]

    g_o[0] = jax.nn.sigmoid(_nt(wc_ref[ZC_G:, :], x) + bg_ref[...]).astype(BF16)


def _proj(x, tabs, w, b_forget, b_gate):
    bsz, seq, dm = x.shape
    tm = ATT_TK
    nk = seq // tm
    h = N_HEADS
    cq_t, sq_t, ck, sk = tabs
    n_gate = N_BRANCH * dm
    bf16 = lambda *s: jax.ShapeDtypeStruct(s, BF16)
    feat = lambda rows: pl.BlockSpec((1, rows, tm), lambda b, i: (b, 0, i))
    head_t = lambda d: pl.BlockSpec((1, h, d, tm), lambda b, i: (b, 0, 0, i))
    head_k = lambda d: pl.BlockSpec((1, h, 1, tm, d), lambda b, i: (b, 0, i, 0, 0))
    head_vt = pl.BlockSpec((1, h, 1, HEAD_DIM, tm), lambda b, i: (b, 0, i, 0, 0))
    one_k = lambda d: pl.BlockSpec((1, 1, tm, d), lambda b, i: (b, i, 0, 0))
    one_vt = pl.BlockSpec((1, 1, HEAD_DIM, tm), lambda b, i: (b, i, 0, 0))
    tok = lambda d: pl.BlockSpec((1, tm, d), lambda b, i: (b, i, 0))
    b_f = jnp.pad(b_forget, (0, FOX_PAD - h))
    consts = list(w) + [
        b_f.reshape(1, FOX_PAD), b_f.reshape(FOX_PAD, 1),
        (jnp.arange(tm)[:, None] >= jnp.arange(tm)[None, :]).astype(F32),
        (jnp.arange(tm)[:, None] <= jnp.arange(tm)[None, :]).astype(F32),
    ] + list(_fox_placement()) + [b_gate.reshape(n_gate, 1)]
    return pl.pallas_call(
        _proj_kernel,
        grid=(bsz, nk),
        in_specs=[tok(dm), feat(h * MLA_QK), feat(h * MLA_QK), tok(MLA_ROPE), tok(MLA_ROPE)]
                 + [_resident(c) for c in consts],
        out_specs=[head_t(MLA_QK), head_k(MLA_QK), head_vt, head_t(FOX_QK), head_k(FOX_QK), head_vt,
                   head_t(HEAD_DIM), one_k(HEAD_DIM), one_vt, head_t(IDX_DIM), one_k(IDX_DIM),
                   feat(h), feat(n_gate)],
        out_shape=[bf16(bsz, h, MLA_QK, seq), bf16(bsz, h, nk, tm, MLA_QK), bf16(bsz, h, nk, HEAD_DIM, tm),
                   bf16(bsz, h, FOX_QK, seq), bf16(bsz, h, nk, tm, FOX_QK), bf16(bsz, h, nk, HEAD_DIM, tm),
                   bf16(bsz, h, HEAD_DIM, seq), bf16(bsz, nk, tm, HEAD_DIM), bf16(bsz, nk, HEAD_DIM, tm),
                   bf16(bsz, h, IDX_DIM, seq), bf16(bsz, nk, tm, IDX_DIM),
                   jax.ShapeDtypeStruct((bsz, h, seq), F32), bf16(bsz, n_gate, seq)],
        scratch_shapes=[pltpu.VMEM((1, FOX_PAD), F32), pltpu.VMEM((FOX_PAD, 1), F32)],
        compiler_params=_cparams(("parallel", "arbitrary")),
        name="proj",
    )(x, cq_t, sq_t, ck, sk, *consts)


def _softmax_step(s, v_t, m_ref, l_ref, acc_ref, h):
    m_old = m_ref[h]
    m_new = jnp.maximum(m_old, jnp.max(s, axis=0, keepdims=True))
    p = jnp.exp2(s - m_new)
    alpha = jnp.exp2(m_old - m_new)
    l_ref[h] = alpha * l_ref[h] + jnp.sum(p, axis=0, keepdims=True)
    acc_ref[h] = alpha * acc_ref[h] + jnp.dot(v_t, p.astype(BF16), preferred_element_type=F32)
    m_ref[h] = m_new


def _causal_mask(tk, tq):
    return lax.broadcasted_iota(jnp.int32, (tk, tq), 0) <= lax.broadcasted_iota(jnp.int32, (tk, tq), 1)


def _pipelined_tiles(j, score, consume):
    score(0, 0)

    def pair(p, carry):
        t = 2 * p
        score(t + 1, 1)
        consume(t, 0, False)
        score(t + 2, 0)
        consume(t + 1, 1, False)
        return carry

    lax.fori_loop(0, j // 2, pair, 0)
    odd = j % 2 == 1

    @pl.when(odd)
    def _():
        score(j, 1)
        consume(j - 1, 0, False)
        consume(j, 1, True)

    @pl.when(jnp.logical_not(odd))
    def _():
        consume(j, 0, True)


def _attn_kernel(qt_ref, k_ref, vt_ref, o_ref, s_ref, m_ref, l_ref, acc_ref):
    j = pl.program_id(1)
    tk, tq = k_ref.shape[3], qt_ref.shape[3]
    m_ref[...] = jnp.full(m_ref.shape, M_INIT, F32)
    l_ref[...] = jnp.zeros(l_ref.shape, F32)
    acc_ref[...] = jnp.zeros(acc_ref.shape, F32)

    def score(i, slot):
        for h in range(N_HEADS):
            s_ref[slot, h] = jnp.dot(k_ref[0, h, i], qt_ref[0, h], preferred_element_type=F32)

    def consume(i, slot, diag):
        for h in range(N_HEADS):
            s = s_ref[slot, h]
            if diag:
                s = jnp.where(_causal_mask(tk, tq), s, MASKED)
            _softmax_step(s, vt_ref[0, h, i], m_ref, l_ref, acc_ref, h)

    _pipelined_tiles(j, score, consume)
    for h in range(N_HEADS):
        o_ref[0, h] = (acc_ref[h] / l_ref[h]).astype(o_ref.dtype)


def _attention(qt, k, vt):
    b, h, dk, s = qt.shape
    nk, tk = k.shape[2], k.shape[3]
    dv = vt.shape[3]
    tq = ATT_TQ
    return pl.pallas_call(
        _attn_kernel,
        grid=(b, s // tq),
        in_specs=[pl.BlockSpec((1, h, dk, tq), lambda bi, j: (bi, 0, 0, j)),
                  pl.BlockSpec((1, h, nk, tk, dk), lambda bi, j: (bi, 0, 0, 0, 0)),
                  pl.BlockSpec((1, h, nk, dv, tk), lambda bi, j: (bi, 0, 0, 0, 0))],
        out_specs=pl.BlockSpec((1, h, dv, tq), lambda bi, j: (bi, 0, 0, j)),
        out_shape=jax.ShapeDtypeStruct((b, h, dv, s), BF16),
        scratch_shapes=[pltpu.VMEM((2, h, tk, tq), F32),
                        pltpu.VMEM((h, 1, tq), F32), pltpu.VMEM((h, 1, tq), F32), pltpu.VMEM((h, dv, tq), F32)],
        compiler_params=_cparams(("parallel", "arbitrary")),
        name="flash_attn",
    )(qt, k, vt)


def _sortable_key(x):
    bits = lax.bitcast_convert_type(x, jnp.int32)
    return bits ^ ((bits >> 31) & 0x7FFFFFFF)


def _count_ge(keys_ref, n_tiles, cand):
    tk, tq = keys_ref.shape[1], keys_ref.shape[2]

    def body(i, c):
        hit = keys_ref[i] >= cand
        return c + jnp.sum(hit.astype(jnp.int32).reshape(tk // 8, 8, tq), axis=0)

    c = lax.fori_loop(0, n_tiles, body, jnp.zeros((8, tq), jnp.int32))
    return jnp.sum(c, axis=0, keepdims=True)


def _count16(ref, n_pairs, cand, strict=False):
    tk, tq = ref.shape[1], ref.shape[2]
    c16 = cand.astype(jnp.int16)

    def body(p, c):
        for t in range(2):
            k = ref[2 * p + t]
            hit = (k > c16) if strict else (k >= c16)
            one = jnp.where(hit, jnp.int16(1), jnp.int16(0))
            for r in range(tk // 16):
                c = c + one[16 * r:16 * (r + 1), :]
        return c

    c = lax.fori_loop(0, n_pairs, body, jnp.zeros((16, tq), jnp.int16))
    return jnp.sum(c.astype(jnp.int32), axis=0, keepdims=True)


def _kth_largest16(ref, n_pairs, kth):
    tq = ref.shape[2]
    cnt = _count16(ref, n_pairs, jnp.zeros((1, tq), jnp.int32))
    prefix = jnp.where(cnt >= kth, jnp.int32(0), jnp.int32(-32768))

    def bit_body(it, prefix):
        cand = prefix + jnp.left_shift(jnp.int32(1), 14 - it)
        return jnp.where(_count16(ref, n_pairs, cand) >= kth, cand, prefix)

    return lax.fori_loop(0, 15, bit_body, prefix)


def _dsa_kernel(sqt_ref, sk_ref, svt_ref, iqt_ref, ik_ref, iw_ref, prow_ref, pcol_ref, o_ref,
                keys_ref, khi_ref, klo_ref, s_ref, m_ref, l_ref, acc_ref, *, topk, slopes):
    j = pl.program_id(1)
    tk, tq = sk_ref.shape[2], sqt_ref.shape[3]
    n_tiles = j + 1
    n_pairs = (n_tiles + 1) // 2
    causal = _causal_mask(tk, tq)

    def store_keys(i, key):
        keys_ref[i] = key
        khi_ref[i] = (key >> 16).astype(jnp.int16)
        klo_ref[i] = ((key & 0xFFFF) - 32768).astype(jnp.int16)

    def score_tile(i, diag):
        ik = ik_ref[0, i]
        sc = jnp.zeros((tk, tq), F32)
        for h in range(N_HEADS):
            lg = jnp.dot(ik, iqt_ref[0, h], preferred_element_type=F32)
            sc = sc + iw_ref[0, h:h + 1, :] * jnp.maximum(lg, 0.0)
        key = _sortable_key(sc)
        if diag:
            key = jnp.where(causal, key, INT_MIN)
        store_keys(i, key)

    def score_body(i, carry):
        score_tile(i, False)
        return carry

    lax.fori_loop(0, j, score_body, 0)
    score_tile(j, True)
    store_keys(n_tiles, jnp.full((tk, tq), INT_MIN, jnp.int32))

    t_hi = _kth_largest16(khi_ref, n_pairs, topk)
    need_lo = topk - _count16(khi_ref, n_pairs, t_hi, strict=True)
    t_hi16 = t_hi.astype(jnp.int16)

    def class_body(i, carry):
        klo_ref[i] = jnp.where(khi_ref[i] == t_hi16, klo_ref[i], jnp.int16(-32768))
        return carry

    lax.fori_loop(0, n_tiles, class_body, 0)
    t_lo = _kth_largest16(klo_ref, n_pairs, need_lo)
    thr = t_hi * 65536 + (t_lo + 32768)

    real = thr > INT_MIN
    cnt_ge = _count_ge(keys_ref, n_tiles, thr)
    tie = jnp.logical_and(real, cnt_ge > topk)
    any_tie = jnp.max(tie.astype(jnp.int32)) > 0

    @pl.when(any_tie)
    def _():
        def count_eq_below(limit):
            def body(i, c):
                idx = i * tk + lax.broadcasted_iota(jnp.int32, (tk, tq), 0)
                hit = jnp.logical_and(keys_ref[i] == thr, idx < limit)
                return c + jnp.sum(hit.astype(jnp.int32).reshape(tk // 8, 8, tq), axis=0)
            c = lax.fori_loop(0, n_tiles, body, jnp.zeros((8, tq), jnp.int32))
            return jnp.sum(c, axis=0, keepdims=True)

        cnt_gt = cnt_ge - count_eq_below(jnp.full((1, tq), n_tiles * tk, jnp.int32))
        need = topk - cnt_gt
        n_bits = max(1, int(math.ceil(math.log2(sk_ref.shape[1] * tk))))

        def idx_body(it, lo):
            t = lo + jnp.left_shift(jnp.int32(1), n_bits - 1 - it)
            return jnp.where(count_eq_below(t) < need, t, lo)

        last = lax.fori_loop(0, n_bits, idx_body, jnp.zeros((1, tq), jnp.int32))

        def drop_body(i, carry):
            idx = i * tk + lax.broadcasted_iota(jnp.int32, (tk, tq), 0)
            k = keys_ref[i]
            drop = jnp.logical_and(jnp.logical_and(k == thr, idx > last), tie)
            keys_ref[i] = jnp.where(drop, INT_MIN, k)
            return carry

        lax.fori_loop(0, n_tiles, drop_body, 0)

    thr_eff = jnp.maximum(thr, INT_MIN + 1)

    m_ref[...] = jnp.full(m_ref.shape, M_INIT, F32)
    l_ref[...] = jnp.zeros(l_ref.shape, F32)
    acc_ref[...] = jnp.zeros(acc_ref.shape, F32)
    p_row = prow_ref[0]

    def score(i, slot):
        k_t = sk_ref[0, i]
        for h in range(N_HEADS):
            s_ref[slot, h] = jnp.dot(k_t, sqt_ref[0, h], preferred_element_type=F32)

    def consume(i, slot, diag):
        del diag
        sel = keys_ref[i] >= thr_eff
        dist = jnp.abs(p_row - pcol_ref[0, i]).astype(F32)
        v_t = svt_ref[0, i]
        for h in range(N_HEADS):
            s = s_ref[slot, h] - (slopes[h] * LOG2E) * dist
            _softmax_step(jnp.where(sel, s, MASKED), v_t, m_ref, l_ref, acc_ref, h)

    _pipelined_tiles(j, score, consume)
    for h in range(N_HEADS):
        o_ref[0, h] = (acc_ref[h] / l_ref[h]).astype(o_ref.dtype)


def _dsa(sqt, sk, svt, iqt, ik, iw_t, pos_row, pos_col, topk, slopes):
    b, h, d, s = sqt.shape
    nk, tk = sk.shape[1], sk.shape[2]
    tq = ATT_TQ
    kv = lambda a: pl.BlockSpec((1,) + a.shape[1:], lambda bi, j: (bi,) + (0,) * (a.ndim - 1))
    return pl.pallas_call(
        functools.partial(_dsa_kernel, topk=topk, slopes=slopes),
        grid=(b, s // tq),
        in_specs=[pl.BlockSpec((1, h, d, tq), lambda bi, j: (bi, 0, 0, j)),
                  kv(sk), kv(svt),
                  pl.BlockSpec((1, h, IDX_DIM, tq), lambda bi, j: (bi, 0, 0, j)),
                  kv(ik),
                  pl.BlockSpec((1, h, tq), lambda bi, j: (bi, 0, j)),
                  pl.BlockSpec((1, 1, tq), lambda bi, j: (bi, 0, j)),
                  kv(pos_col)],
        out_specs=pl.BlockSpec((1, h, d, tq), lambda bi, j: (bi, 0, 0, j)),
        out_shape=jax.ShapeDtypeStruct((b, h, d, s), BF16),
        scratch_shapes=[pltpu.VMEM((nk + 1, tk, tq), jnp.int32),
                        pltpu.VMEM((nk + 1, tk, tq), jnp.int16), pltpu.VMEM((nk + 1, tk, tq), jnp.int16),
                        pltpu.VMEM((2, h, tk, tq), F32),
                        pltpu.VMEM((h, 1, tq), F32), pltpu.VMEM((h, 1, tq), F32), pltpu.VMEM((h, d, tq), F32)],
        compiler_params=_cparams(("parallel", "arbitrary")),
        name="dsa",
    )(sqt, sk, svt, iqt, ik, iw_t, pos_row, pos_col)


def _layernorm(r, g, b):
    mu = jnp.mean(r, -1, keepdims=True)
    d = r - mu
    var = jnp.mean(jnp.square(d), -1, keepdims=True)
    return d * lax.rsqrt(var + LN_EPS) * g + b


def _merge_kernel(o1_ref, o2_ref, o3_ref, g_ref, x_ref, wb_ref, wo_ref, lng_ref, lnb_ref, xo_ref, xb_ref, *, alpha):
    tm, dm = x_ref.shape[1], x_ref.shape[2]
    merged = jnp.zeros((dm, tm), F32)
    for n, o_ref in enumerate((o1_ref, o2_ref, o3_ref)):
        o_t = o_ref[0].reshape(N_HEADS * HEAD_DIM, tm)
        proj = jnp.dot(wb_ref[n], o_t, preferred_element_type=F32)
        merged = merged + g_ref[0, n * dm:(n + 1) * dm, :].astype(F32) * proj
    y = jnp.dot(merged.T.astype(BF16), wo_ref[...], preferred_element_type=F32)
    out = _layernorm(alpha * x_ref[0] + y, lng_ref[...], lnb_ref[...])
    xo_ref[0] = out
    xb_ref[0] = out.astype(BF16)


def _merge(o1, o2, o3, gates, x, wb_t, wo, lng, lnb, alpha):
    bsz, seq, dm = x.shape
    tm = ATT_TQ
    o_spec = pl.BlockSpec((1, N_HEADS, HEAD_DIM, tm), lambda b, i: (b, 0, 0, i))
    blk = pl.BlockSpec((1, tm, dm), lambda b, i: (b, i, 0))
    return pl.pallas_call(
        functools.partial(_merge_kernel, alpha=alpha),
        grid=(bsz, seq // tm),
        in_specs=[o_spec, o_spec, o_spec, pl.BlockSpec((1, N_BRANCH * dm, tm), lambda b, i: (b, 0, i)), blk,
                  _resident(wb_t), _resident(wo), _resident(lng), _resident(lnb)],
        out_specs=[blk, blk],
        out_shape=[jax.ShapeDtypeStruct((bsz, seq, dm), F32), jax.ShapeDtypeStruct((bsz, seq, dm), BF16)],
        compiler_params=_cparams(("parallel", "parallel")),
        name="merge",
    )(o1, o2, o3, gates, x, wb_t, wo, lng, lnb)


def _ffn_kernel(x_ref, wg_ref, wv_ref, cw_ref, cb_ref, wd_ref, lng_ref, lnb_ref, xo_ref, xb_ref,
                tail_ref, gv_ref, y_ref, *, alpha):
    tm = x_ref.shape[1]
    n_chunks = wg_ref.shape[0]
    x = x_ref[0]
    xb = x.astype(BF16)
    row = lax.broadcasted_iota(jnp.int32, (tm, FF_CHUNK), 0)
    y_ref[...] = jnp.zeros(y_ref.shape, F32)

    @pl.when(pl.program_id(1) == 0)
    def _():
        tail_ref[...] = jnp.zeros(tail_ref.shape, F32)

    def up(c, slot):
        gv_ref[slot, 0] = jnp.dot(xb, wg_ref[c], preferred_element_type=F32)
        gv_ref[slot, 1] = jnp.dot(xb, wv_ref[c], preferred_element_type=F32)

    def down(c, slot):
        g = gv_ref[slot, 0]
        tail = tail_ref[c]
        g1 = jnp.where(row < 1, tail[7:8, :], pltpu.roll(g, 1, axis=0))
        g2 = jnp.where(row < 2, jnp.where(row < 1, tail[6:7, :], tail[7:8, :]), pltpu.roll(g, 2, axis=0))
        cw = cw_ref[c]
        conv = cb_ref[c] + g2 * cw[0:1, :] + g1 * cw[1:2, :] + g * cw[2:3, :]
        gelu = 0.5 * conv * (1.0 + lax.erf(conv * (2.0 ** -0.5)))
        hid = (gelu * gv_ref[slot, 1]).astype(BF16)
        y_ref[...] += jnp.dot(hid, wd_ref[c], preferred_element_type=F32)
        tail_ref[c] = g[tm - 8:tm, :]

    up(0, 0)

    def pair(p, carry):
        c = 2 * p
        up(c + 1, 1)
        down(c, 0)
        up(c + 2, 0)
        down(c + 1, 1)
        return carry

    lax.fori_loop(0, (n_chunks - 1) // 2, pair, 0)
    if n_chunks % 2 == 0:
        up(n_chunks - 1, 1)
        down(n_chunks - 2, 0)
        down(n_chunks - 1, 1)
    else:
        down(n_chunks - 1, 0)
    out = _layernorm(alpha * x + y_ref[...], lng_ref[...], lnb_ref[...])
    xo_ref[0] = out
    xb_ref[0] = out.astype(BF16)


def _ffn(x, wg, wv, cw, cb, wd, lng, lnb, alpha, tm):
    b, s, dm = x.shape
    n_chunks = wg.shape[0]
    blk = pl.BlockSpec((1, tm, dm), lambda bi, i: (bi, i, 0))
    return pl.pallas_call(
        functools.partial(_ffn_kernel, alpha=alpha),
        grid=(b, s // tm),
        in_specs=[blk] + [_resident(a) for a in (wg, wv, cw, cb, wd, lng, lnb)],
        out_specs=[blk, blk],
        out_shape=[jax.ShapeDtypeStruct((b, s, dm), F32), jax.ShapeDtypeStruct((b, s, dm), BF16)],
        scratch_shapes=[pltpu.VMEM((n_chunks, 8, FF_CHUNK), F32), pltpu.VMEM((2, 2, tm, FF_CHUNK), F32),
                        pltpu.VMEM((tm, dm), F32)],
        compiler_params=_cparams(("parallel", "arbitrary")),
        name="ffn",
    )(x, wg, wv, cw, cb, wd, lng, lnb)


def _split_offsets(d_model):
    sizes = (MLA_Q_RANK, MLA_KV_RANK, MLA_ROPE, 512, 512, 512, N_HEADS, 512, HEAD_DIM, HEAD_DIM,
             N_HEADS * IDX_DIM, IDX_DIM, N_HEADS, N_BRANCH * d_model)
    offs, o = [], 0
    for w in sizes:
        offs.append((o, o + w))
        o += w
    return offs


def _proj_weights(w_in, q_norm, kv_norm, w_uq, w_ukv):
    d = w_in.shape[0]
    (c_q, c_kv, k_r, f_q, f_k, f_v, f_l, s_q, s_k, s_v, i_q, i_k, i_w, gate) = [
        w_in[:, a:b] for a, b in _split_offsets(d)]
    half = MLA_ROPE // 2
    k_r_rot = jnp.concatenate([-k_r[:, half:], k_r[:, :half]], axis=1)
    pad_cols = lambda a, n: jnp.pad(a, ((0, 0), (0, n - a.shape[1])))
    f_k_heads = jnp.pad(f_k.reshape(d, N_HEADS, HEAD_DIM), ((0, 0), (0, 0), (0, LANES - HEAD_DIM))).reshape(d, -1)
    w_row = jnp.concatenate([c_kv, c_q, k_r, k_r_rot, pad_cols(f_l, LANES), f_k_heads,
                             pad_cols(s_k, LANES), pad_cols(i_k, LANES)], axis=1).astype(BF16)
    w_col = jnp.concatenate([f_q, f_v, s_q, s_v, i_q, pad_cols(i_w, 16), pad_cols(f_l, 16), gate], axis=1)
    w_col_t = w_col.T.astype(BF16)

    r = w_uq.shape[0]
    wq = w_uq.reshape(r, N_HEADS, MLA_QK)
    rope = wq[:, :, MLA_NOPE:]
    wq_rot = jnp.concatenate([jnp.zeros((r, N_HEADS, MLA_NOPE), w_uq.dtype), -rope[:, :, half:], rope[:, :, :half]], axis=2)
    wkv = w_ukv.reshape(w_ukv.shape[0], N_HEADS, MLA_NOPE + HEAD_DIM)
    wk = jnp.pad(wkv[:, :, :MLA_NOPE], ((0, 0), (0, 0), (0, LANES - MLA_NOPE))).reshape(wkv.shape[0], -1)
    wv = wkv[:, :, MLA_NOPE:].reshape(wkv.shape[0], -1)
    place = jnp.zeros((MLA_ROPE, N_HEADS, LANES), F32)
    place = place.at[jnp.arange(MLA_ROPE), :, MLA_NOPE + jnp.arange(MLA_ROPE)].set(1.0).reshape(MLA_ROPE, -1)
    return (w_row, w_col_t, q_norm.reshape(1, -1), kv_norm.reshape(1, -1),
            w_uq.T.astype(BF16), wq_rot.reshape(r, -1).T.astype(BF16), wk.astype(BF16), wv.T.astype(BF16),
            place.astype(BF16))


def _fox_placement():
    h = jnp.arange(N_HEADS)
    pk = jnp.zeros((3, FOX_PAD, N_HEADS * LANES), F32)
    pq = jnp.zeros((3, N_HEADS * FOX_PAD, FOX_PAD), F32)
    one_k = jnp.zeros((1, N_HEADS * LANES), F32)
    one_q = jnp.zeros((N_HEADS * FOX_PAD, 1), F32)
    for n in range(3):
        pk = pk.at[n, h, LANES * h + HEAD_DIM + 3 + n].set(-1.0)
        pq = pq.at[n, FOX_PAD * h + n, h].set(1.0)
        one_k = one_k.at[0, LANES * h + HEAD_DIM + n].set(1.0)
        one_q = one_q.at[FOX_PAD * h + 3 + n, 0].set(1.0)
    return pk.astype(BF16), one_k, pq.astype(BF16), one_q


def _rope_tables(positions):
    half = MLA_ROPE // 2
    inv = ROPE_THETA ** (-jnp.arange(half, dtype=F32) / half)
    ang = positions.astype(F32)[..., None] * inv
    cos, sin = jnp.cos(ang), jnp.sin(ang)
    b, s = positions.shape
    q_scale = (MLA_QK ** -0.5) * LOG2E
    cos2 = jnp.concatenate([cos, cos], -1)
    sin2 = jnp.concatenate([sin, sin], -1)
    ones = jnp.ones((b, s, MLA_NOPE), F32)
    cq = jnp.tile(jnp.concatenate([ones, cos2], -1) * q_scale, (1, 1, N_HEADS))
    sq = jnp.tile(jnp.concatenate([0.0 * ones, sin2], -1) * q_scale, (1, 1, N_HEADS))
    return cq.transpose(0, 2, 1), sq.transpose(0, 2, 1), cos2, sin2


def kernel(x, positions, w_in, b_gate, b_forget, mla_q_norm, mla_kv_norm, mla_w_uq, mla_w_ukv, w_branch, w_out,
           ln1_g, ln1_b, ffn_w_up, ffn_conv_w, ffn_conv_b, ffn_w_down, ln2_g, ln2_b):
    bsz, seq, dm = x.shape
    depth = w_in.shape[0]
    d_ff = ffn_w_down.shape[1]
    tk = ATT_TK
    nk = seq // tk
    topk = min(DSA_MAX_TOPK, seq // 4)
    alpha = (2 * depth) ** 0.25
    slopes = tuple(2.0 ** (-8.0 * i / N_HEADS) for i in range(1, N_HEADS + 1))
    n_chunks = d_ff // FF_CHUNK
    chunks = lambda w: w.reshape(w.shape[0], n_chunks, FF_CHUNK).transpose(1, 0, 2)

    tabs = _rope_tables(positions)
    pos_row = positions.reshape(bsz, 1, seq)
    pos_col = positions.reshape(bsz, nk, tk, 1)

    xf = x
    xb = x.astype(BF16)
    for l in range(depth):
        (mq, mk, mv, fq, fk, fv, sq, sk, sv, iq, ik, iw, gates) = _proj(
            xb, tabs, _proj_weights(w_in[l], mla_q_norm[l], mla_kv_norm[l], mla_w_uq[l], mla_w_ukv[l]),
            b_forget[l], b_gate[l])
        o_mla = _attention(mq, mk, mv)
        o_fox = _attention(fq, fk, fv)
        o_dsa = _dsa(sq, sk, sv, iq, ik, iw, pos_row, pos_col, topk, slopes)
        xf, xb = _merge(o_mla, o_fox, o_dsa, gates, xf,
                        w_branch[l].transpose(0, 2, 1).astype(BF16), w_out[l].astype(BF16),
                        ln1_g[l].reshape(1, -1), ln1_b[l].reshape(1, -1), alpha)
        w_up = ffn_w_up[l]
        xf, xb = _ffn(xf, chunks(w_up[:, :d_ff]).astype(BF16), chunks(w_up[:, d_ff:]).astype(BF16),
                      chunks(ffn_conv_w[l]), chunks(ffn_conv_b[l].reshape(1, -1)),
                      ffn_w_down[l].reshape(n_chunks, FF_CHUNK, dm).astype(BF16),
                      ln2_g[l].reshape(1, -1), ln2_b[l].reshape(1, -1), alpha, min(512, seq))
    return xf
```

```python
import functools
import math

import jax
import jax.numpy as jnp
from jax import lax
from jax.experimental import pallas as pl
from jax.experimental.pallas import tpu as pltpu

F32 = jnp.float32
BF16 = jnp.bfloat16

N_HEADS = 8
HEAD_DIM = 64
MLA_Q_RANK = 192
MLA_KV_RANK = 128
MLA_NOPE = 64
MLA_ROPE = 32
MLA_QK = MLA_NOPE + MLA_ROPE
IDX_DIM = 32
N_BRANCH = 3
CONV_WIDTH = 3
ROPE_THETA = 10000.0
DSA_MAX_TOPK = 256
LN_EPS = 1e-5
RMS_EPS = 1e-6
LOG2E = math.log2(math.e)
LANES = 128

ZR_CKV, ZR_CQ, ZR_KR, ZR_KRR = 0, 128, 320, 352
ZR_FL = 384
ZR_FK = 512
ZR_SK = 1536
ZR_IK = 1664
ZR_W = 1792
ZC_FQ, ZC_FV, ZC_SQ, ZC_SV, ZC_IQ, ZC_IW, ZC_FL, ZC_G = 0, 512, 1024, 1536, 1600, 1856, 1872, 1888
FOX_QK = 80
FOX_PAD = 16
DSA_QK = 80
DSA_V = 80
POS_LIMIT = 1 << 16

ATT_TQ = 256
ATT_TK = 256
MASKED = -1e30
M_INIT = -5e29
FAR = 1e33
INT_MIN = -2 ** 31
COUNT_GROUP = 4
FF_CHUNK = 256
VMEM_LIMIT = 56 * 1024 * 1024


def _cparams(sem):
    return pltpu.CompilerParams(dimension_semantics=sem, vmem_limit_bytes=VMEM_LIMIT)


def _resident(a):
    return pl.BlockSpec(a.shape, lambda *_: (0,) * a.ndim, pipeline_mode=pl.Buffered(1))


def _nt(w, x):
    return lax.dot_general(w, x, (((1,), (1,)), ((), ())), preferred_element_type=F32)


def _split3(c):
    hi = c.astype(BF16)
    r1 = c - hi.astype(F32)
    mid = r1.astype(BF16)
    lo = (r1 - mid.astype(F32)).astype(BF16)
    return hi, mid, lo


def _proj_kernel(x_ref, cq_ref, sq_ref, ck_ref, sk_ref, prow_ref, pcol_ref, wr_ref, wc_ref, qn_ref, kvn_ref,
                 wq_ref, wqr_ref, wk_ref, wv_ref, e_ref, bfr_ref, bfc_ref, tril_ref, triu_ref, pk_ref, onek_ref,
                 pq_ref, oneq_ref, slope_ref, aq_ref, ek_ref, bg_ref,
                 mq_o, mk_o, mv_o, fq_o, fk_o, fv_o, sq_o, sk_o, sv_o, iq_o, ik_o, iw_o, g_o, cr_ref, cc_ref):
    tm = x_ref.shape[1]

    @pl.when(pl.program_id(1) == 0)
    def _():
        cr_ref[...] = jnp.zeros(cr_ref.shape, F32)
        cc_ref[...] = jnp.zeros(cc_ref.shape, F32)

    x = x_ref[0]
    zr = jnp.dot(x, wr_ref[...], preferred_element_type=F32)
    zc = _nt(wc_ref[0:ZC_G, :], x)

    c_kv = zr[:, ZR_CKV:ZR_CKV + MLA_KV_RANK]
    c_q = zr[:, ZR_CQ:ZR_CQ + MLA_Q_RANK]
    nq = (c_q * lax.rsqrt(jnp.mean(jnp.square(c_q), -1, keepdims=True) + RMS_EPS) * qn_ref[...]).astype(BF16)
    nkv = (c_kv * lax.rsqrt(jnp.mean(jnp.square(c_kv), -1, keepdims=True) + RMS_EPS) * kvn_ref[...]).astype(BF16)
    q_m = _nt(wq_ref[...], nq) * cq_ref[0] + _nt(wqr_ref[...], nq) * sq_ref[0]
    k_rope = (zr[:, ZR_KR:ZR_KR + MLA_ROPE] * ck_ref[0] + zr[:, ZR_KRR:ZR_KRR + MLA_ROPE] * sk_ref[0]).astype(BF16)
    k_m = (jnp.dot(nkv, wk_ref[...], preferred_element_type=F32)
           + jnp.dot(k_rope, e_ref[...], preferred_element_type=F32))
    v_m = _nt(wv_ref[...], nkv)

    lf_r = jax.nn.log_sigmoid(zr[:, ZR_FL:ZR_FL + FOX_PAD] + bfr_ref[...])
    cum_r = jnp.dot(tril_ref[...], lf_r, preferred_element_type=F32, precision=lax.Precision.HIGHEST) + cr_ref[...]
    lf_c = jax.nn.log_sigmoid(zc[ZC_FL:ZC_FL + FOX_PAD] + bfc_ref[...])
    cum_c = jnp.dot(lf_c, triu_ref[...], preferred_element_type=F32, precision=lax.Precision.HIGHEST) + cc_ref[...]
    cr_ref[...] = cum_r[tm - 1:tm, :]
    cc_ref[...] = cum_c[:, tm - 1:tm]
    k_bias = onek_ref[...]
    for n, term in enumerate(_split3(cum_r * LOG2E)):
        k_bias = k_bias + jnp.dot(term, pk_ref[n], preferred_element_type=F32)
    q_bias = oneq_ref[...]
    for n, term in enumerate(_split3(cum_c * LOG2E)):
        q_bias = q_bias + jnp.dot(pq_ref[n], term, preferred_element_type=F32)
    f_k = zr[:, ZR_FK:ZR_FK + N_HEADS * LANES] + k_bias
    q_scale = HEAD_DIM ** -0.5 * LOG2E
    f_q = zc[ZC_FQ:ZC_FQ + N_HEADS * HEAD_DIM] * q_scale
    s_q = zc[ZC_SQ:ZC_SQ + N_HEADS * HEAD_DIM] * q_scale

    d_bias = aq_ref[...]
    for n, term in enumerate(_split3(slope_ref[...] * prow_ref[0])):
        d_bias = d_bias + jnp.dot(pq_ref[n], term, preferred_element_type=F32)
    p_k = pcol_ref[0]
    p_hi = p_k.astype(BF16).astype(F32)
    s_k = (zr[:, ZR_SK:ZR_SK + LANES] + ek_ref[0:1, :] + p_hi * ek_ref[1:2, :] + (p_k - p_hi) * ek_ref[2:3, :])

    for h in range(N_HEADS):
        mq_o[0, h] = q_m[MLA_QK * h:MLA_QK * (h + 1)].astype(BF16)
        mk_o[0, h, 0] = k_m[:, LANES * h:LANES * h + MLA_QK].astype(BF16)
        mv_o[0, h, 0] = v_m[HEAD_DIM * h:HEAD_DIM * (h + 1)].astype(BF16)
        fq_o[0, h, 0:HEAD_DIM, :] = f_q[HEAD_DIM * h:HEAD_DIM * (h + 1)].astype(BF16)
        fq_o[0, h, HEAD_DIM:FOX_QK, :] = q_bias[FOX_PAD * h:FOX_PAD * (h + 1)].astype(BF16)
        fk_o[0, h, 0] = f_k[:, LANES * h:LANES * h + FOX_QK].astype(BF16)
        fv_o[0, h, 0] = zc[ZC_FV + HEAD_DIM * h:ZC_FV + HEAD_DIM * (h + 1)].astype(BF16)
        sq_o[0, h, 0:HEAD_DIM, :] = s_q[HEAD_DIM * h:HEAD_DIM * (h + 1)].astype(BF16)
        sq_o[0, h, HEAD_DIM:DSA_QK, :] = d_bias[FOX_PAD * h:FOX_PAD * (h + 1)].astype(BF16)
        iq_o[0, h] = zc[ZC_IQ + IDX_DIM * h:ZC_IQ + IDX_DIM * (h + 1)].astype(BF16)

    sk_o[0, 0] = s_k[:, 0:DSA_QK].astype(BF16)
    sv_o[0, 0, 0:HEAD_DIM, :] = zc[ZC_SV:ZC_SV + HEAD_DIM].astype(BF16)
    ones_row = lax.broadcasted_iota(jnp.int32, (DSA_V - HEAD_DIM, tm), 0) == 0
    sv_o[0, 0, HEAD_DIM:DSA_V, :] = jnp.where(ones_row, 1.0, 0.0).astype(BF16)
    ik_o[0, 0] = zr[:, ZR_IK:ZR_IK + IDX_DIM].astype(BF16)
    iw_o[0] = zc[ZC_IW:ZC_IW + N_HEADS]

    g_o[0] = jax.nn.sigmoid(_nt(wc_ref[ZC_G:, :], x) + bg_ref[...]).astype(BF16)


def _proj(x, tabs, w, b_forget, b_gate, slopes):
    bsz, seq, dm = x.shape
    tm = ATT_TK
    nk = seq // tm
    h = N_HEADS
    cq_t, sq_t, ck, sk, pos_row, pos_col = tabs
    n_gate = N_BRANCH * dm
    bf16 = lambda *s: jax.ShapeDtypeStruct(s, BF16)
    feat = lambda rows: pl.BlockSpec((1, rows, tm), lambda b, i: (b, 0, i))
    head_t = lambda d: pl.BlockSpec((1, h, d, tm), lambda b, i: (b, 0, 0, i))
    head_k = lambda d: pl.BlockSpec((1, h, 1, tm, d), lambda b, i: (b, 0, i, 0, 0))
    head_vt = pl.BlockSpec((1, h, 1, HEAD_DIM, tm), lambda b, i: (b, 0, i, 0, 0))
    one_k = lambda d: pl.BlockSpec((1, 1, tm, d), lambda b, i: (b, i, 0, 0))
    one_vt = pl.BlockSpec((1, 1, DSA_V, tm), lambda b, i: (b, i, 0, 0))
    tok = lambda d: pl.BlockSpec((1, tm, d), lambda b, i: (b, i, 0))
    b_f = jnp.pad(b_forget, (0, FOX_PAD - h))
    consts = list(w) + [
        b_f.reshape(1, FOX_PAD), b_f.reshape(FOX_PAD, 1),
        (jnp.arange(tm)[:, None] >= jnp.arange(tm)[None, :]).astype(F32),
        (jnp.arange(tm)[:, None] <= jnp.arange(tm)[None, :]).astype(F32),
    ] + list(_fox_placement()) + list(_alibi_placement(slopes)) + [b_gate.reshape(n_gate, 1)]
    return pl.pallas_call(
        _proj_kernel,
        grid=(bsz, nk),
        in_specs=[tok(dm), feat(h * MLA_QK), feat(h * MLA_QK), tok(MLA_ROPE), tok(MLA_ROPE), feat(1), tok(1)]
                 + [_resident(c) for c in consts],
        out_specs=[head_t(MLA_QK), head_k(MLA_QK), head_vt, head_t(FOX_QK), head_k(FOX_QK), head_vt,
                   head_t(DSA_QK), one_k(DSA_QK), one_vt, head_t(IDX_DIM), one_k(IDX_DIM),
                   feat(h), feat(n_gate)],
        out_shape=[bf16(bsz, h, MLA_QK, seq), bf16(bsz, h, nk, tm, MLA_QK), bf16(bsz, h, nk, HEAD_DIM, tm),
                   bf16(bsz, h, FOX_QK, seq), bf16(bsz, h, nk, tm, FOX_QK), bf16(bsz, h, nk, HEAD_DIM, tm),
                   bf16(bsz, h, DSA_QK, seq), bf16(bsz, nk, tm, DSA_QK), bf16(bsz, nk, DSA_V, tm),
                   bf16(bsz, h, IDX_DIM, seq), bf16(bsz, nk, tm, IDX_DIM),
                   jax.ShapeDtypeStruct((bsz, h, seq), F32), bf16(bsz, n_gate, seq)],
        scratch_shapes=[pltpu.VMEM((1, FOX_PAD), F32), pltpu.VMEM((FOX_PAD, 1), F32)],
        compiler_params=_cparams(("parallel", "arbitrary")),
        name="proj",
    )(x, cq_t, sq_t, ck, sk, pos_row, pos_col, *consts)


def _softmax_step(s, v_t, m_ref, l_ref, acc_ref, h):
    m_old = m_ref[h]
    m_new = jnp.maximum(m_old, jnp.max(s, axis=0, keepdims=True))
    p = jnp.exp2(s - m_new)
    alpha = jnp.exp2(m_old - m_new)
    if l_ref is not None:
        l_ref[h] = alpha * l_ref[h] + jnp.sum(p, axis=0, keepdims=True)
    acc_ref[h] = alpha * acc_ref[h] + jnp.dot(v_t, p.astype(BF16), preferred_element_type=F32)
    m_ref[h] = m_new


def _causal_mask(tk, tq):
    return lax.broadcasted_iota(jnp.int32, (tk, tq), 0) <= lax.broadcasted_iota(jnp.int32, (tk, tq), 1)


def _pipelined_tiles(j, score, consume):
    score(0, 0)

    def pair(p, carry):
        t = 2 * p
        score(t + 1, 1)
        consume(t, 0, False)
        score(t + 2, 0)
        consume(t + 1, 1, False)
        return carry

    lax.fori_loop(0, j // 2, pair, 0)
    odd = j % 2 == 1

    @pl.when(odd)
    def _():
        score(j, 1)
        consume(j - 1, 0, False)
        consume(j, 1, True)

    @pl.when(jnp.logical_not(odd))
    def _():
        consume(j, 0, True)


def _attn_kernel(qt_ref, k_ref, vt_ref, o_ref, s_ref, m_ref, l_ref, acc_ref):
    j = pl.program_id(1)
    tk, tq = k_ref.shape[3], qt_ref.shape[3]
    m_ref[...] = jnp.full(m_ref.shape, M_INIT, F32)
    l_ref[...] = jnp.zeros(l_ref.shape, F32)
    acc_ref[...] = jnp.zeros(acc_ref.shape, F32)

    def score(i, slot):
        for h in range(N_HEADS):
            s_ref[slot, h] = jnp.dot(k_ref[0, h, i], qt_ref[0, h], preferred_element_type=F32)

    def consume(i, slot, diag):
        for h in range(N_HEADS):
            s = s_ref[slot, h]
            if diag:
                s = jnp.where(_causal_mask(tk, tq), s, MASKED)
            _softmax_step(s, vt_ref[0, h, i], m_ref, l_ref, acc_ref, h)

    _pipelined_tiles(j, score, consume)
    for h in range(N_HEADS):
        o_ref[0, h] = (acc_ref[h] / l_ref[h]).astype(o_ref.dtype)


def _attention(qt, k, vt):
    b, h, dk, s = qt.shape
    nk, tk = k.shape[2], k.shape[3]
    dv = vt.shape[3]
    tq = ATT_TQ
    return pl.pallas_call(
        _attn_kernel,
        grid=(b, s // tq),
        in_specs=[pl.BlockSpec((1, h, dk, tq), lambda bi, j: (bi, 0, 0, j)),
                  pl.BlockSpec((1, h, nk, tk, dk), lambda bi, j: (bi, 0, 0, 0, 0)),
                  pl.BlockSpec((1, h, nk, dv, tk), lambda bi, j: (bi, 0, 0, 0, 0))],
        out_specs=pl.BlockSpec((1, h, dv, tq), lambda bi, j: (bi, 0, 0, j)),
        out_shape=jax.ShapeDtypeStruct((b, h, dv, s), BF16),
        scratch_shapes=[pltpu.VMEM((2, h, tk, tq), F32),
                        pltpu.VMEM((h, 1, tq), F32), pltpu.VMEM((h, 1, tq), F32), pltpu.VMEM((h, dv, tq), F32)],
        compiler_params=_cparams(("parallel", "arbitrary")),
        name="flash_attn",
    )(qt, k, vt)


def _sortable_key(x):
    bits = lax.bitcast_convert_type(x, jnp.int32)
    return bits ^ ((bits >> 31) & 0x7FFFFFFF)


def _count_ge(keys_ref, n_tiles, cand):
    tk, tq = keys_ref.shape[1], keys_ref.shape[2]

    def body(i, c):
        hit = keys_ref[i] >= cand
        return c + jnp.sum(hit.astype(jnp.int32).reshape(tk // 8, 8, tq), axis=0)

    c = lax.fori_loop(0, n_tiles, body, jnp.zeros((8, tq), jnp.int32))
    return jnp.sum(c, axis=0, keepdims=True)


def _count16(ref, n_groups, cand, strict=False):
    tk, tq = ref.shape[1], ref.shape[2]
    c16 = cand.astype(jnp.int16)

    def body(p, c):
        for t in range(COUNT_GROUP):
            k = ref[COUNT_GROUP * p + t]
            hit = (k > c16) if strict else (k >= c16)
            one = jnp.where(hit, jnp.int16(1), jnp.int16(0))
            for r in range(tk // 16):
                c = c + one[16 * r:16 * (r + 1), :]
        return c

    c = lax.fori_loop(0, n_groups, body, jnp.zeros((16, tq), jnp.int16))
    return jnp.sum(c.astype(jnp.int32), axis=0, keepdims=True)


def _kth_largest16(ref, n_groups, kth):
    tq = ref.shape[2]
    cnt = _count16(ref, n_groups, jnp.zeros((1, tq), jnp.int32))
    prefix = jnp.where(cnt >= kth, jnp.int32(0), jnp.int32(-32768))

    def bit_body(it, prefix):
        cand = prefix + jnp.left_shift(jnp.int32(1), 14 - it)
        return jnp.where(_count16(ref, n_groups, cand) >= kth, cand, prefix)

    return lax.fori_loop(0, 15, bit_body, prefix)


def _dsa_kernel(flag_ref, sqt_ref, sk_ref, svt_ref, iqt_ref, ik_ref, iw_ref, prow_ref, pcol_ref, o_ref,
                keys_ref, khi_ref, klo_ref, s_ref, m_ref, acc_ref, *, topk, slopes):
    j = pl.program_id(1)
    tk, tq = sk_ref.shape[2], sqt_ref.shape[3]
    n_tiles = j + 1
    n_groups = (n_tiles + COUNT_GROUP - 1) // COUNT_GROUP
    causal = _causal_mask(tk, tq)

    def store_keys(i, key):
        keys_ref[i] = key
        khi_ref[i] = (key >> 16).astype(jnp.int16)
        klo_ref[i] = ((key & 0xFFFF) - 32768).astype(jnp.int16)

    def score_tile(i, diag):
        ik = ik_ref[0, i]
        sc = jnp.zeros((tk, tq), F32)
        for h in range(N_HEADS):
            lg = jnp.dot(ik, iqt_ref[0, h], preferred_element_type=F32)
            sc = sc + iw_ref[0, h:h + 1, :] * jnp.maximum(lg, 0.0)
        key = _sortable_key(sc)
        if diag:
            key = jnp.where(causal, key, INT_MIN)
        store_keys(i, key)

    def score_body(i, carry):
        score_tile(i, False)
        return carry

    lax.fori_loop(0, j, score_body, 0)
    score_tile(j, True)
    for t in range(COUNT_GROUP - 1):
        store_keys(n_tiles + t, jnp.full((tk, tq), INT_MIN, jnp.int32))

    t_hi = _kth_largest16(khi_ref, n_groups, topk)
    need_lo = topk - _count16(khi_ref, n_groups, t_hi, strict=True)
    t_hi16 = t_hi.astype(jnp.int16)

    def class_body(i, carry):
        klo_ref[i] = jnp.where(khi_ref[i] == t_hi16, klo_ref[i], jnp.int16(-32768))
        return carry

    lax.fori_loop(0, n_tiles, class_body, 0)
    t_lo = _kth_largest16(klo_ref, n_groups, need_lo)
    thr = t_hi * 65536 + (t_lo + 32768)

    real = thr > INT_MIN
    cnt_ge = _count_ge(keys_ref, n_tiles, thr)
    tie = jnp.logical_and(real, cnt_ge > topk)
    any_tie = jnp.max(tie.astype(jnp.int32)) > 0

    @pl.when(any_tie)
    def _():
        def count_eq_below(limit):
            def body(i, c):
                idx = i * tk + lax.broadcasted_iota(jnp.int32, (tk, tq), 0)
                hit = jnp.logical_and(keys_ref[i] == thr, idx < limit)
                return c + jnp.sum(hit.astype(jnp.int32).reshape(tk // 8, 8, tq), axis=0)
            c = lax.fori_loop(0, n_tiles, body, jnp.zeros((8, tq), jnp.int32))
            return jnp.sum(c, axis=0, keepdims=True)

        cnt_gt = cnt_ge - count_eq_below(jnp.full((1, tq), n_tiles * tk, jnp.int32))
        need = topk - cnt_gt
        n_bits = max(1, int(math.ceil(math.log2(sk_ref.shape[1] * tk))))

        def idx_body(it, lo):
            t = lo + jnp.left_shift(jnp.int32(1), n_bits - 1 - it)
            return jnp.where(count_eq_below(t) < need, t, lo)

        last = lax.fori_loop(0, n_bits, idx_body, jnp.zeros((1, tq), jnp.int32))

        def drop_body(i, carry):
            idx = i * tk + lax.broadcasted_iota(jnp.int32, (tk, tq), 0)
            k = keys_ref[i]
            drop = jnp.logical_and(jnp.logical_and(k == thr, idx > last), tie)
            keys_ref[i] = jnp.where(drop, INT_MIN, k)
            return carry

        lax.fori_loop(0, n_tiles, drop_body, 0)

    thr_eff = jnp.maximum(thr, INT_MIN + 1)

    m_ref[...] = jnp.full(m_ref.shape, M_INIT, F32)
    acc_ref[...] = jnp.zeros(acc_ref.shape, F32)
    p_row = prow_ref[0]

    def attend(linear_alibi):
        def score(i, slot):
            k_t = sk_ref[0, i] if linear_alibi else sk_ref[0, i][:, 0:HEAD_DIM]
            for h in range(N_HEADS):
                q_t = sqt_ref[0, h] if linear_alibi else sqt_ref[0, h][0:HEAD_DIM, :]
                s_ref[slot, h] = jnp.dot(k_t, q_t, preferred_element_type=F32)

        def consume(i, slot, diag):
            del diag
            sel = keys_ref[i] >= thr_eff
            v_t = svt_ref[0, i]
            if linear_alibi:
                bias = jnp.where(sel, 0.0, MASKED)
            else:
                bias = jnp.where(sel, jnp.abs(p_row - pcol_ref[0, i]).astype(F32), FAR)
            for h in range(N_HEADS):
                s = s_ref[slot, h] + bias if linear_alibi else s_ref[slot, h] - (slopes[h] * LOG2E) * bias
                _softmax_step(s, v_t, m_ref, None, acc_ref, h)

        _pipelined_tiles(j, score, consume)

    linear = flag_ref[pl.program_id(0)] != 0

    @pl.when(linear)
    def _():
        attend(True)

    @pl.when(jnp.logical_not(linear))
    def _():
        attend(False)

    for h in range(N_HEADS):
        acc = acc_ref[h]
        o_ref[0, h] = (acc[0:HEAD_DIM] / acc[HEAD_DIM:HEAD_DIM + 1]).astype(o_ref.dtype)


def _dsa(linear_alibi, sqt, sk, svt, iqt, ik, iw_t, pos_row, pos_col, topk, slopes):
    b, h, dq, s = sqt.shape
    nk, tk = sk.shape[1], sk.shape[2]
    d = HEAD_DIM
    tq = ATT_TQ
    n_key_tiles = nk + COUNT_GROUP - 1
    kv = lambda a: pl.BlockSpec((1,) + a.shape[1:], lambda bi, j: (bi,) + (0,) * (a.ndim - 1))
    return pl.pallas_call(
        functools.partial(_dsa_kernel, topk=topk, slopes=slopes),
        grid=(b, s // tq),
        in_specs=[pl.BlockSpec(memory_space=pltpu.SMEM),
                  pl.BlockSpec((1, h, dq, tq), lambda bi, j: (bi, 0, 0, j)),
                  kv(sk), kv(svt),
                  pl.BlockSpec((1, h, IDX_DIM, tq), lambda bi, j: (bi, 0, 0, j)),
                  kv(ik),
                  pl.BlockSpec((1, h, tq), lambda bi, j: (bi, 0, j)),
                  pl.BlockSpec((1, 1, tq), lambda bi, j: (bi, 0, j)),
                  kv(pos_col)],
        out_specs=pl.BlockSpec((1, h, d, tq), lambda bi, j: (bi, 0, 0, j)),
        out_shape=jax.ShapeDtypeStruct((b, h, d, s), BF16),
        scratch_shapes=[pltpu.VMEM((n_key_tiles, tk, tq), jnp.int32),
                        pltpu.VMEM((n_key_tiles, tk, tq), jnp.int16), pltpu.VMEM((n_key_tiles, tk, tq), jnp.int16),
                        pltpu.VMEM((2, h, tk, tq), F32),
                        pltpu.VMEM((h, 1, tq), F32), pltpu.VMEM((h, svt.shape[2], tq), F32)],
        compiler_params=_cparams(("parallel", "arbitrary")),
        name="dsa",
    )(linear_alibi, sqt, sk, svt, iqt, ik, iw_t, pos_row, pos_col)


def _layernorm(r, g, b):
    mu = jnp.mean(r, -1, keepdims=True)
    d = r - mu
    var = jnp.mean(jnp.square(d), -1, keepdims=True)
    return d * lax.rsqrt(var + LN_EPS) * g + b


def _merge_kernel(o1_ref, o2_ref, o3_ref, g_ref, x_ref, wb_ref, wo_ref, lng_ref, lnb_ref, xo_ref, xb_ref, *, alpha):
    tm, dm = x_ref.shape[1], x_ref.shape[2]
    merged = jnp.zeros((dm, tm), F32)
    for n, o_ref in enumerate((o1_ref, o2_ref, o3_ref)):
        o_t = o_ref[0].reshape(N_HEADS * HEAD_DIM, tm)
        proj = jnp.dot(wb_ref[n], o_t, preferred_element_type=F32)
        merged = merged + g_ref[0, n * dm:(n + 1) * dm, :].astype(F32) * proj
    y = jnp.dot(merged.T.astype(BF16), wo_ref[...], preferred_element_type=F32)
    out = _layernorm(alpha * x_ref[0] + y, lng_ref[...], lnb_ref[...])
    xo_ref[0] = out
    xb_ref[0] = out.astype(BF16)


def _merge(o1, o2, o3, gates, x, wb_t, wo, lng, lnb, alpha):
    bsz, seq, dm = x.shape
    tm = ATT_TQ
    o_spec = pl.BlockSpec((1, N_HEADS, HEAD_DIM, tm), lambda b, i: (b, 0, 0, i))
    blk = pl.BlockSpec((1, tm, dm), lambda b, i: (b, i, 0))
    return pl.pallas_call(
        functools.partial(_merge_kernel, alpha=alpha),
        grid=(bsz, seq // tm),
        in_specs=[o_spec, o_spec, o_spec, pl.BlockSpec((1, N_BRANCH * dm, tm), lambda b, i: (b, 0, i)), blk,
                  _resident(wb_t), _resident(wo), _resident(lng), _resident(lnb)],
        out_specs=[blk, blk],
        out_shape=[jax.ShapeDtypeStruct((bsz, seq, dm), F32), jax.ShapeDtypeStruct((bsz, seq, dm), BF16)],
        compiler_params=_cparams(("parallel", "parallel")),
        name="merge",
    )(o1, o2, o3, gates, x, wb_t, wo, lng, lnb)


def _ffn_kernel(x_ref, wg_ref, wv_ref, cw_ref, cb_ref, wd_ref, lng_ref, lnb_ref, xo_ref, xb_ref,
                tail_ref, gv_ref, y_ref, *, alpha):
    tm = x_ref.shape[1]
    n_chunks = wg_ref.shape[0]
    x = x_ref[0]
    xb = x.astype(BF16)
    row = lax.broadcasted_iota(jnp.int32, (tm, FF_CHUNK), 0)
    y_ref[...] = jnp.zeros(y_ref.shape, F32)

    @pl.when(pl.program_id(1) == 0)
    def _():
        tail_ref[...] = jnp.zeros(tail_ref.shape, F32)

    def up(c, slot):
        gv_ref[slot, 0] = jnp.dot(xb, wg_ref[c], preferred_element_type=F32)
        gv_ref[slot, 1] = jnp.dot(xb, wv_ref[c], preferred_element_type=F32)

    def down(c, slot):
        g = gv_ref[slot, 0]
        tail = tail_ref[c]
        g1 = jnp.where(row < 1, tail[7:8, :], pltpu.roll(g, 1, axis=0))
        g2 = jnp.where(row < 2, jnp.where(row < 1, tail[6:7, :], tail[7:8, :]), pltpu.roll(g, 2, axis=0))
        cw = cw_ref[c]
        conv = cb_ref[c] + g2 * cw[0:1, :] + g1 * cw[1:2, :] + g * cw[2:3, :]
        gelu = 0.5 * conv * (1.0 + lax.erf(conv * (2.0 ** -0.5)))
        hid = (gelu * gv_ref[slot, 1]).astype(BF16)
        y_ref[...] += jnp.dot(hid, wd_ref[c], preferred_element_type=F32)
        tail_ref[c] = g[tm - 8:tm, :]

    up(0, 0)

    def pair(p, carry):
        c = 2 * p
        up(c + 1, 1)
        down(c, 0)
        up(c + 2, 0)
        down(c + 1, 1)
        return carry

    lax.fori_loop(0, (n_chunks - 1) // 2, pair, 0)
    if n_chunks % 2 == 0:
        up(n_chunks - 1, 1)
        down(n_chunks - 2, 0)
        down(n_chunks - 1, 1)
    else:
        down(n_chunks - 1, 0)
    out = _layernorm(alpha * x + y_ref[...], lng_ref[...], lnb_ref[...])
    xo_ref[0] = out
    xb_ref[0] = out.astype(BF16)


def _ffn(x, wg, wv, cw, cb, wd, lng, lnb, alpha, tm):
    b, s, dm = x.shape
    n_chunks = wg.shape[0]
    blk = pl.BlockSpec((1, tm, dm), lambda bi, i: (bi, i, 0))
    return pl.pallas_call(
        functools.partial(_ffn_kernel, alpha=alpha),
        grid=(b, s // tm),
        in_specs=[blk] + [_resident(a) for a in (wg, wv, cw, cb, wd, lng, lnb)],
        out_specs=[blk, blk],
        out_shape=[jax.ShapeDtypeStruct((b, s, dm), F32), jax.ShapeDtypeStruct((b, s, dm), BF16)],
        scratch_shapes=[pltpu.VMEM((n_chunks, 8, FF_CHUNK), F32), pltpu.VMEM((2, 2, tm, FF_CHUNK), F32),
                        pltpu.VMEM((tm, dm), F32)],
        compiler_params=_cparams(("parallel", "arbitrary")),
        name="ffn",
    )(x, wg, wv, cw, cb, wd, lng, lnb)


def _split_offsets(d_model):
    sizes = (MLA_Q_RANK, MLA_KV_RANK, MLA_ROPE, 512, 512, 512, N_HEADS, 512, HEAD_DIM, HEAD_DIM,
             N_HEADS * IDX_DIM, IDX_DIM, N_HEADS, N_BRANCH * d_model)
    offs, o = [], 0
    for w in sizes:
        offs.append((o, o + w))
        o += w
    return offs


def _proj_weights(w_in, q_norm, kv_norm, w_uq, w_ukv):
    d = w_in.shape[0]
    (c_q, c_kv, k_r, f_q, f_k, f_v, f_l, s_q, s_k, s_v, i_q, i_k, i_w, gate) = [
        w_in[:, a:b] for a, b in _split_offsets(d)]
    half = MLA_ROPE // 2
    k_r_rot = jnp.concatenate([-k_r[:, half:], k_r[:, :half]], axis=1)
    pad_cols = lambda a, n: jnp.pad(a, ((0, 0), (0, n - a.shape[1])))
    f_k_heads = jnp.pad(f_k.reshape(d, N_HEADS, HEAD_DIM), ((0, 0), (0, 0), (0, LANES - HEAD_DIM))).reshape(d, -1)
    w_row = jnp.concatenate([c_kv, c_q, k_r, k_r_rot, pad_cols(f_l, LANES), f_k_heads,
                             pad_cols(s_k, LANES), pad_cols(i_k, LANES)], axis=1).astype(BF16)
    w_col = jnp.concatenate([f_q, f_v, s_q, s_v, i_q, pad_cols(i_w, 16), pad_cols(f_l, 16), gate], axis=1)
    w_col_t = w_col.T.astype(BF16)

    r = w_uq.shape[0]
    wq = w_uq.reshape(r, N_HEADS, MLA_QK)
    rope = wq[:, :, MLA_NOPE:]
    wq_rot = jnp.concatenate([jnp.zeros((r, N_HEADS, MLA_NOPE), w_uq.dtype), -rope[:, :, half:], rope[:, :, :half]], axis=2)
    wkv = w_ukv.reshape(w_ukv.shape[0], N_HEADS, MLA_NOPE + HEAD_DIM)
    wk = jnp.pad(wkv[:, :, :MLA_NOPE], ((0, 0), (0, 0), (0, LANES - MLA_NOPE))).reshape(wkv.shape[0], -1)
    wv = wkv[:, :, MLA_NOPE:].reshape(wkv.shape[0], -1)
    place = jnp.zeros((MLA_ROPE, N_HEADS, LANES), F32)
    place = place.at[jnp.arange(MLA_ROPE), :, MLA_NOPE + jnp.arange(MLA_ROPE)].set(1.0).reshape(MLA_ROPE, -1)
    return (w_row, w_col_t, q_norm.reshape(1, -1), kv_norm.reshape(1, -1),
            w_uq.T.astype(BF16), wq_rot.reshape(r, -1).T.astype(BF16), wk.astype(BF16), wv.T.astype(BF16),
            place.astype(BF16))


def _fox_placement():
    h = jnp.arange(N_HEADS)
    pk = jnp.zeros((3, FOX_PAD, N_HEADS * LANES), F32)
    pq = jnp.zeros((3, N_HEADS * FOX_PAD, FOX_PAD), F32)
    one_k = jnp.zeros((1, N_HEADS * LANES), F32)
    one_q = jnp.zeros((N_HEADS * FOX_PAD, 1), F32)
    for n in range(3):
        pk = pk.at[n, h, LANES * h + HEAD_DIM + 3 + n].set(-1.0)
        pq = pq.at[n, FOX_PAD * h + n, h].set(1.0)
        one_k = one_k.at[0, LANES * h + HEAD_DIM + n].set(1.0)
        one_q = one_q.at[FOX_PAD * h + 3 + n, 0].set(1.0)
    return pk.astype(BF16), one_k, pq.astype(BF16), one_q


def _alibi_placement(slopes):
    a = jnp.asarray(slopes, F32) * LOG2E
    neg_a = jnp.pad(-a, (0, FOX_PAD - N_HEADS)).reshape(FOX_PAD, 1)
    terms = jnp.stack([t.astype(F32) for t in _split3(a)], axis=1)
    a_q = jnp.zeros((N_HEADS, FOX_PAD), F32).at[:, 3:6].set(terms).at[:, 6:9].set(terms).reshape(-1, 1)
    e_k = jnp.zeros((3, LANES), F32)
    for n in range(3):
        e_k = e_k.at[n, HEAD_DIM + 3 * n:HEAD_DIM + 3 * n + 3].set(1.0)
    return neg_a, a_q, e_k


def _rope_tables(positions):
    half = MLA_ROPE // 2
    inv = ROPE_THETA ** (-jnp.arange(half, dtype=F32) / half)
    ang = positions.astype(F32)[..., None] * inv
    cos, sin = jnp.cos(ang), jnp.sin(ang)
    b, s = positions.shape
    q_scale = (MLA_QK ** -0.5) * LOG2E
    cos2 = jnp.concatenate([cos, cos], -1)
    sin2 = jnp.concatenate([sin, sin], -1)
    ones = jnp.ones((b, s, MLA_NOPE), F32)
    cq = jnp.tile(jnp.concatenate([ones, cos2], -1) * q_scale, (1, 1, N_HEADS))
    sq = jnp.tile(jnp.concatenate([0.0 * ones, sin2], -1) * q_scale, (1, 1, N_HEADS))
    return cq.transpose(0, 2, 1), sq.transpose(0, 2, 1), cos2, sin2


def kernel(x, positions, w_in, b_gate, b_forget, mla_q_norm, mla_kv_norm, mla_w_uq, mla_w_ukv, w_branch, w_out,
           ln1_g, ln1_b, ffn_w_up, ffn_conv_w, ffn_conv_b, ffn_w_down, ln2_g, ln2_b):
    bsz, seq, dm = x.shape
    depth = w_in.shape[0]
    d_ff = ffn_w_down.shape[1]
    tk = ATT_TK
    nk = seq // tk
    topk = min(DSA_MAX_TOPK, seq // 4)
    alpha = (2 * depth) ** 0.25
    slopes = tuple(2.0 ** (-8.0 * i / N_HEADS) for i in range(1, N_HEADS + 1))
    n_chunks = d_ff // FF_CHUNK
    chunks = lambda w: w.reshape(w.shape[0], n_chunks, FF_CHUNK).transpose(1, 0, 2)

    pos_row = positions.reshape(bsz, 1, seq)
    pos_col = positions.reshape(bsz, nk, tk, 1)
    tabs = _rope_tables(positions) + (pos_row.astype(F32), positions.reshape(bsz, seq, 1).astype(F32))
    linear_alibi = (jnp.all(positions[:, 1:] >= positions[:, :-1], axis=1)
                    & (jnp.min(positions, axis=1) >= 0) & (jnp.max(positions, axis=1) < POS_LIMIT)).astype(jnp.int32)

    xf = x
    xb = x.astype(BF16)
    for l in range(depth):
        (mq, mk, mv, fq, fk, fv, sq, sk, sv, iq, ik, iw, gates) = _proj(
            xb, tabs, _proj_weights(w_in[l], mla_q_norm[l], mla_kv_norm[l], mla_w_uq[l], mla_w_ukv[l]),
            b_forget[l], b_gate[l], slopes)
        o_mla = _attention(mq, mk, mv)
        o_fox = _attention(fq, fk, fv)
        o_dsa = _dsa(linear_alibi, sq, sk, sv, iq, ik, iw, pos_row, pos_col, topk, slopes)
        xf, xb = _merge(o_mla, o_fox, o_dsa, gates, xf,
                        w_branch[l].transpose(0, 2, 1).astype(BF16), w_out[l].astype(BF16),
                        ln1_g[l].reshape(1, -1), ln1_b[l].reshape(1, -1), alpha)
        w_up = ffn_w_up[l]
        xf, xb = _ffn(xf, chunks(w_up[:, :d_ff]).astype(BF16), chunks(w_up[:, d_ff:]).astype(BF16),
                      chunks(ffn_conv_w[l]), chunks(ffn_conv_b[l].reshape(1, -1)),
                      ffn_w_down[l].reshape(n_chunks, FF_CHUNK, dm).astype(BF16),
                      ln2_g[l].reshape(1, -1), ln2_b[l].reshape(1, -1), alpha, min(512, seq))
    return xf
```

```python
import functools
import math

import jax
import jax.numpy as jnp
from jax import lax
from jax.experimental import pallas as pl
from jax.experimental.pallas import tpu as pltpu

F32 = jnp.float32
BF16 = jnp.bfloat16

N_HEADS = 8
HEAD_DIM = 64
MLA_Q_RANK = 192
MLA_KV_RANK = 128
MLA_NOPE = 64
MLA_ROPE = 32
MLA_QK = MLA_NOPE + MLA_ROPE
IDX_DIM = 32
N_BRANCH = 3
CONV_WIDTH = 3
ROPE_THETA = 10000.0
DSA_MAX_TOPK = 256
LN_EPS = 1e-5
RMS_EPS = 1e-6
LOG2E = math.log2(math.e)
LANES = 128

ZR_CKV, ZR_CQ, ZR_KR, ZR_KRR = 0, 128, 320, 352
ZR_FL = 384
ZR_FK = 512
ZR_SK = 1536
ZR_IK = 1664
ZR_W = 1792
ZC_FQ, ZC_FV, ZC_SQ, ZC_SV, ZC_IQ, ZC_IW, ZC_FL, ZC_G = 0, 512, 1024, 1536, 1600, 1856, 1872, 1888
FOX_QK = 80
FOX_PAD = 16
DSA_QK = 80
ATT_V = 80
ATT_LOOKAHEAD = 3
POS_LIMIT = 1 << 16

ATT_TQ = 256
ATT_TK = 256
MASKED = -1e30
M_INIT = -5e29
FAR = 1e33
INT_MIN = -2 ** 31
COUNT_GROUP = 4
FF_CHUNK = 256
VMEM_LIMIT = 56 * 1024 * 1024


def _cparams(sem):
    return pltpu.CompilerParams(dimension_semantics=sem, vmem_limit_bytes=VMEM_LIMIT)


def _resident(a):
    return pl.BlockSpec(a.shape, lambda *_: (0,) * a.ndim, pipeline_mode=pl.Buffered(1))


def _nt(w, x):
    return lax.dot_general(w, x, (((1,), (1,)), ((), ())), preferred_element_type=F32)


def _split3(c):
    hi = c.astype(BF16)
    r1 = c - hi.astype(F32)
    mid = r1.astype(BF16)
    lo = (r1 - mid.astype(F32)).astype(BF16)
    return hi, mid, lo


def _proj_kernel(x_ref, cq_ref, sq_ref, ck_ref, sk_ref, prow_ref, pcol_ref, wr_ref, wc_ref, qn_ref, kvn_ref,
                 wq_ref, wqr_ref, wk_ref, wv_ref, e_ref, bfr_ref, bfc_ref, tril_ref, triu_ref, pk_ref, onek_ref,
                 pq_ref, oneq_ref, slope_ref, aq_ref, ek_ref, bg_ref,
                 mq_o, mk_o, mv_o, fq_o, fk_o, fv_o, sq_o, sk_o, sv_o, iq_o, ik_o, iw_o, g_o, cr_ref, cc_ref):
    tm = x_ref.shape[1]

    @pl.when(pl.program_id(1) == 0)
    def _():
        cr_ref[...] = jnp.zeros(cr_ref.shape, F32)
        cc_ref[...] = jnp.zeros(cc_ref.shape, F32)

    x = x_ref[0]
    zr = jnp.dot(x, wr_ref[...], preferred_element_type=F32)
    zc = _nt(wc_ref[0:ZC_G, :], x)

    c_kv = zr[:, ZR_CKV:ZR_CKV + MLA_KV_RANK]
    c_q = zr[:, ZR_CQ:ZR_CQ + MLA_Q_RANK]
    nq = (c_q * lax.rsqrt(jnp.mean(jnp.square(c_q), -1, keepdims=True) + RMS_EPS) * qn_ref[...]).astype(BF16)
    nkv = (c_kv * lax.rsqrt(jnp.mean(jnp.square(c_kv), -1, keepdims=True) + RMS_EPS) * kvn_ref[...]).astype(BF16)
    q_m = _nt(wq_ref[...], nq) * cq_ref[0] + _nt(wqr_ref[...], nq) * sq_ref[0]
    k_rope = (zr[:, ZR_KR:ZR_KR + MLA_ROPE] * ck_ref[0] + zr[:, ZR_KRR:ZR_KRR + MLA_ROPE] * sk_ref[0]).astype(BF16)
    k_m = (jnp.dot(nkv, wk_ref[...], preferred_element_type=F32)
           + jnp.dot(k_rope, e_ref[...], preferred_element_type=F32))
    v_m = _nt(wv_ref[...], nkv)

    lf_r = jax.nn.log_sigmoid(zr[:, ZR_FL:ZR_FL + FOX_PAD] + bfr_ref[...])
    cum_r = jnp.dot(tril_ref[...], lf_r, preferred_element_type=F32, precision=lax.Precision.HIGHEST) + cr_ref[...]
    lf_c = jax.nn.log_sigmoid(zc[ZC_FL:ZC_FL + FOX_PAD] + bfc_ref[...])
    cum_c = jnp.dot(lf_c, triu_ref[...], preferred_element_type=F32, precision=lax.Precision.HIGHEST) + cc_ref[...]
    cr_ref[...] = cum_r[tm - 1:tm, :]
    cc_ref[...] = cum_c[:, tm - 1:tm]
    k_bias = onek_ref[...]
    for n, term in enumerate(_split3(cum_r * LOG2E)):
        k_bias = k_bias + jnp.dot(term, pk_ref[n], preferred_element_type=F32)
    q_bias = oneq_ref[...]
    for n, term in enumerate(_split3(cum_c * LOG2E)):
        q_bias = q_bias + jnp.dot(pq_ref[n], term, preferred_element_type=F32)
    f_k = zr[:, ZR_FK:ZR_FK + N_HEADS * LANES] + k_bias
    q_scale = HEAD_DIM ** -0.5 * LOG2E
    f_q = zc[ZC_FQ:ZC_FQ + N_HEADS * HEAD_DIM] * q_scale
    s_q = zc[ZC_SQ:ZC_SQ + N_HEADS * HEAD_DIM] * q_scale

    d_bias = aq_ref[...]
    for n, term in enumerate(_split3(slope_ref[...] * prow_ref[0])):
        d_bias = d_bias + jnp.dot(pq_ref[n], term, preferred_element_type=F32)
    p_k = pcol_ref[0]
    p_hi = p_k.astype(BF16).astype(F32)
    s_k = (zr[:, ZR_SK:ZR_SK + LANES] + ek_ref[0:1, :] + p_hi * ek_ref[1:2, :] + (p_k - p_hi) * ek_ref[2:3, :])

    ones_row = lax.broadcasted_iota(jnp.int32, (ATT_V - HEAD_DIM, tm), 0) == 0
    ones_blk = jnp.where(ones_row, 1.0, 0.0).astype(BF16)
    for h in range(N_HEADS):
        mq_o[0, h] = q_m[MLA_QK * h:MLA_QK * (h + 1)].astype(BF16)
        mk_o[0, h, 0] = k_m[:, LANES * h:LANES * h + MLA_QK].astype(BF16)
        mv_o[0, h, 0, 0:HEAD_DIM, :] = v_m[HEAD_DIM * h:HEAD_DIM * (h + 1)].astype(BF16)
        mv_o[0, h, 0, HEAD_DIM:ATT_V, :] = ones_blk
        fq_o[0, h, 0:HEAD_DIM, :] = f_q[HEAD_DIM * h:HEAD_DIM * (h + 1)].astype(BF16)
        fq_o[0, h, HEAD_DIM:FOX_QK, :] = q_bias[FOX_PAD * h:FOX_PAD * (h + 1)].astype(BF16)
        fk_o[0, h, 0] = f_k[:, LANES * h:LANES * h + FOX_QK].astype(BF16)
        fv_o[0, h, 0, 0:HEAD_DIM, :] = zc[ZC_FV + HEAD_DIM * h:ZC_FV + HEAD_DIM * (h + 1)].astype(BF16)
        fv_o[0, h, 0, HEAD_DIM:ATT_V, :] = ones_blk
        sq_o[0, h, 0:HEAD_DIM, :] = s_q[HEAD_DIM * h:HEAD_DIM * (h + 1)].astype(BF16)
        sq_o[0, h, HEAD_DIM:DSA_QK, :] = d_bias[FOX_PAD * h:FOX_PAD * (h + 1)].astype(BF16)
        iq_o[0, h] = zc[ZC_IQ + IDX_DIM * h:ZC_IQ + IDX_DIM * (h + 1)].astype(BF16)

    sk_o[0, 0] = s_k[:, 0:DSA_QK].astype(BF16)
    sv_o[0, 0, 0:HEAD_DIM, :] = zc[ZC_SV:ZC_SV + HEAD_DIM].astype(BF16)
    sv_o[0, 0, HEAD_DIM:ATT_V, :] = ones_blk
    ik_o[0, 0] = zr[:, ZR_IK:ZR_IK + IDX_DIM].astype(BF16)
    iw_o[0] = zc[ZC_IW:ZC_IW + N_HEADS]

    g_o[0] = jax.nn.sigmoid(_nt(wc_ref[ZC_G:, :], x) + bg_ref[...]).astype(BF16)


def _proj(x, tabs, w, b_forget, b_gate, slopes):
    bsz, seq, dm = x.shape
    tm = ATT_TK
    nk = seq // tm
    h = N_HEADS
    cq_t, sq_t, ck, sk, pos_row, pos_col = tabs
    n_gate = N_BRANCH * dm
    bf16 = lambda *s: jax.ShapeDtypeStruct(s, BF16)
    feat = lambda rows: pl.BlockSpec((1, rows, tm), lambda b, i: (b, 0, i))
    head_t = lambda d: pl.BlockSpec((1, h, d, tm), lambda b, i: (b, 0, 0, i))
    head_k = lambda d: pl.BlockSpec((1, h, 1, tm, d), lambda b, i: (b, 0, i, 0, 0))
    head_vt = pl.BlockSpec((1, h, 1, ATT_V, tm), lambda b, i: (b, 0, i, 0, 0))
    one_k = lambda d: pl.BlockSpec((1, 1, tm, d), lambda b, i: (b, i, 0, 0))
    one_vt = pl.BlockSpec((1, 1, ATT_V, tm), lambda b, i: (b, i, 0, 0))
    tok = lambda d: pl.BlockSpec((1, tm, d), lambda b, i: (b, i, 0))
    b_f = jnp.pad(b_forget, (0, FOX_PAD - h))
    consts = list(w) + [
        b_f.reshape(1, FOX_PAD), b_f.reshape(FOX_PAD, 1),
        (jnp.arange(tm)[:, None] >= jnp.arange(tm)[None, :]).astype(F32),
        (jnp.arange(tm)[:, None] <= jnp.arange(tm)[None, :]).astype(F32),
    ] + list(_fox_placement()) + list(_alibi_placement(slopes)) + [b_gate.reshape(n_gate, 1)]
    return pl.pallas_call(
        _proj_kernel,
        grid=(bsz, nk),
        in_specs=[tok(dm), feat(h * MLA_QK), feat(h * MLA_QK), tok(MLA_ROPE), tok(MLA_ROPE), feat(1), tok(1)]
                 + [_resident(c) for c in consts],
        out_specs=[head_t(MLA_QK), head_k(MLA_QK), head_vt, head_t(FOX_QK), head_k(FOX_QK), head_vt,
                   head_t(DSA_QK), one_k(DSA_QK), one_vt, head_t(IDX_DIM), one_k(IDX_DIM),
                   feat(h), feat(n_gate)],
        out_shape=[bf16(bsz, h, MLA_QK, seq), bf16(bsz, h, nk, tm, MLA_QK), bf16(bsz, h, nk, ATT_V, tm),
                   bf16(bsz, h, FOX_QK, seq), bf16(bsz, h, nk, tm, FOX_QK), bf16(bsz, h, nk, ATT_V, tm),
                   bf16(bsz, h, DSA_QK, seq), bf16(bsz, nk, tm, DSA_QK), bf16(bsz, nk, ATT_V, tm),
                   bf16(bsz, h, IDX_DIM, seq), bf16(bsz, nk, tm, IDX_DIM),
                   jax.ShapeDtypeStruct((bsz, h, seq), F32), bf16(bsz, n_gate, seq)],
        scratch_shapes=[pltpu.VMEM((1, FOX_PAD), F32), pltpu.VMEM((FOX_PAD, 1), F32)],
        compiler_params=_cparams(("parallel", "arbitrary")),
        name="proj",
    )(x, cq_t, sq_t, ck, sk, pos_row, pos_col, *consts)


def _softmax_step(s, v_t, m_ref, acc_ref, h):
    m_old = m_ref[h]
    m_new = jnp.maximum(m_old, jnp.max(s, axis=0, keepdims=True))
    p = jnp.exp2(s - m_new)
    acc_ref[h] = jnp.exp2(m_old - m_new) * acc_ref[h] + jnp.dot(v_t, p.astype(BF16), preferred_element_type=F32)
    m_ref[h] = m_new


def _normalized(acc):
    return acc[0:HEAD_DIM] / acc[HEAD_DIM:HEAD_DIM + 1]


def _causal_mask(tk, tq):
    return lax.broadcasted_iota(jnp.int32, (tk, tq), 0) <= lax.broadcasted_iota(jnp.int32, (tk, tq), 1)


def _pipelined_tiles(j, matmuls, consume, prepare=None, lookahead=N_HEADS):
    def run(cur, slot, diag, nxt=None):
        shared = prepare(cur) if prepare is not None else None
        if nxt is not None:
            for h in range(lookahead):
                matmuls(nxt, 1 - slot, h)
        for h in range(N_HEADS):
            consume(cur, slot, diag, h, shared)
            if nxt is not None and h + lookahead < N_HEADS:
                matmuls(nxt, 1 - slot, h + lookahead)

    for h in range(N_HEADS):
        matmuls(0, 0, h)

    def pair(p, carry):
        t = 2 * p
        run(t, 0, False, t + 1)
        run(t + 1, 1, False, t + 2)
        return carry

    lax.fori_loop(0, j // 2, pair, 0)
    odd = j % 2 == 1

    @pl.when(odd)
    def _():
        run(j - 1, 0, False, j)
        run(j, 1, True)

    @pl.when(jnp.logical_not(odd))
    def _():
        run(j, 0, True)


def _attn_kernel(qt_ref, k_ref, vt_ref, o_ref, s_ref, m_ref, acc_ref):
    j = pl.program_id(1)
    tk, tq = k_ref.shape[3], qt_ref.shape[3]
    m_ref[...] = jnp.full(m_ref.shape, M_INIT, F32)
    acc_ref[...] = jnp.zeros(acc_ref.shape, F32)

    def score(i, slot, h):
        s_ref[slot, h] = jnp.dot(k_ref[0, h, i], qt_ref[0, h], preferred_element_type=F32)

    def consume(i, slot, diag, h, shared):
        s = s_ref[slot, h]
        if diag:
            s = jnp.where(_causal_mask(tk, tq), s, MASKED)
        _softmax_step(s, vt_ref[0, h, i], m_ref, acc_ref, h)

    _pipelined_tiles(j, score, consume, lookahead=ATT_LOOKAHEAD)
    for h in range(N_HEADS):
        o_ref[0, h] = _normalized(acc_ref[h]).astype(o_ref.dtype)


def _attention(qt, k, vt):
    b, h, dk, s = qt.shape
    nk, tk = k.shape[2], k.shape[3]
    tq = ATT_TQ
    return pl.pallas_call(
        _attn_kernel,
        grid=(b, s // tq),
        in_specs=[pl.BlockSpec((1, h, dk, tq), lambda bi, j: (bi, 0, 0, j)),
                  pl.BlockSpec((1, h, nk, tk, dk), lambda bi, j: (bi, 0, 0, 0, 0)),
                  pl.BlockSpec((1, h, nk, ATT_V, tk), lambda bi, j: (bi, 0, 0, 0, 0))],
        out_specs=pl.BlockSpec((1, h, HEAD_DIM, tq), lambda bi, j: (bi, 0, 0, j)),
        out_shape=jax.ShapeDtypeStruct((b, h, HEAD_DIM, s), BF16),
        scratch_shapes=[pltpu.VMEM((2, h, tk, tq), F32),
                        pltpu.VMEM((h, 1, tq), F32), pltpu.VMEM((h, ATT_V, tq), F32)],
        compiler_params=_cparams(("parallel", "arbitrary")),
        name="flash_attn",
    )(qt, k, vt)


def _sortable_key(x):
    bits = lax.bitcast_convert_type(x, jnp.int32)
    return jnp.where(bits == INT_MIN, 0, bits ^ ((bits >> 31) & 0x7FFFFFFF))


def _count_ge(keys_ref, n_tiles, cand):
    tk, tq = keys_ref.shape[1], keys_ref.shape[2]

    def body(i, c):
        hit = keys_ref[i] >= cand
        return c + jnp.sum(hit.astype(jnp.int32).reshape(tk // 8, 8, tq), axis=0)

    c = lax.fori_loop(0, n_tiles, body, jnp.zeros((8, tq), jnp.int32))
    return jnp.sum(c, axis=0, keepdims=True)


def _count16(ref, n_groups, cand, strict=False):
    tk, tq = ref.shape[1], ref.shape[2]
    c16 = cand.astype(jnp.int16)

    def body(p, c):
        for t in range(COUNT_GROUP):
            k = ref[COUNT_GROUP * p + t]
            hit = (k > c16) if strict else (k >= c16)
            one = jnp.where(hit, jnp.int16(1), jnp.int16(0))
            for r in range(tk // 16):
                c = c + one[16 * r:16 * (r + 1), :]
        return c

    c = lax.fori_loop(0, n_groups, body, jnp.zeros((16, tq), jnp.int16))
    return jnp.sum(c.astype(jnp.int32), axis=0, keepdims=True)


def _kth_largest16(ref, n_groups, kth):
    tq = ref.shape[2]
    cnt = _count16(ref, n_groups, jnp.zeros((1, tq), jnp.int32))
    prefix = jnp.where(cnt >= kth, jnp.int32(0), jnp.int32(-32768))

    def bit_body(it, prefix):
        cand = prefix + jnp.left_shift(jnp.int32(1), 14 - it)
        return jnp.where(_count16(ref, n_groups, cand) >= kth, cand, prefix)

    return lax.fori_loop(0, 15, bit_body, prefix)


def _dsa_kernel(flag_ref, sqt_ref, sk_ref, svt_ref, iqt_ref, ik_ref, iw_ref, prow_ref, pcol_ref, o_ref,
                keys_ref, khi_ref, klo_ref, s_ref, m_ref, acc_ref, *, topk, slopes):
    j = pl.program_id(1)
    tk, tq = sk_ref.shape[2], sqt_ref.shape[3]
    n_tiles = j + 1
    n_groups = (n_tiles + COUNT_GROUP - 1) // COUNT_GROUP
    causal = _causal_mask(tk, tq)

    def store_keys(i, key):
        keys_ref[i] = key
        khi_ref[i] = (key >> 16).astype(jnp.int16)
        klo_ref[i] = ((key & 0xFFFF) - 32768).astype(jnp.int16)

    def index_logits(i, slot, h):
        s_ref[slot, h] = jnp.dot(ik_ref[0, i], iqt_ref[0, h], preferred_element_type=F32)

    def index_keys(i, slot, diag, h, score):
        term = iw_ref[0, h:h + 1, :] * jnp.maximum(s_ref[slot, h], 0.0)
        score[0] = term if h == 0 else score[0] + term
        if h == N_HEADS - 1:
            key = _sortable_key(score[0])
            if diag:
                key = jnp.where(causal, key, INT_MIN)
            store_keys(i, key)

    _pipelined_tiles(j, index_logits, index_keys, prepare=lambda i: [None])
    for t in range(COUNT_GROUP - 1):
        store_keys(n_tiles + t, jnp.full((tk, tq), INT_MIN, jnp.int32))

    t_hi = _kth_largest16(khi_ref, n_groups, topk)
    need_lo = topk - _count16(khi_ref, n_groups, t_hi, strict=True)
    t_hi16 = t_hi.astype(jnp.int16)

    def class_body(i, carry):
        klo_ref[i] = jnp.where(khi_ref[i] == t_hi16, klo_ref[i], jnp.int16(-32768))
        return carry

    lax.fori_loop(0, n_tiles, class_body, 0)
    t_lo = _kth_largest16(klo_ref, n_groups, need_lo)
    thr = t_hi * 65536 + (t_lo + 32768)

    real = thr > INT_MIN
    cnt_ge = _count_ge(keys_ref, n_tiles, thr)
    tie = jnp.logical_and(real, cnt_ge > topk)
    any_tie = jnp.max(tie.astype(jnp.int32)) > 0

    @pl.when(any_tie)
    def _():
        def count_eq_below(limit):
            def body(i, c):
                idx = i * tk + lax.broadcasted_iota(jnp.int32, (tk, tq), 0)
                hit = jnp.logical_and(keys_ref[i] == thr, idx < limit)
                return c + jnp.sum(hit.astype(jnp.int32).reshape(tk // 8, 8, tq), axis=0)
            c = lax.fori_loop(0, n_tiles, body, jnp.zeros((8, tq), jnp.int32))
            return jnp.sum(c, axis=0, keepdims=True)

        cnt_gt = cnt_ge - count_eq_below(jnp.full((1, tq), n_tiles * tk, jnp.int32))
        need = topk - cnt_gt
        n_bits = max(1, int(math.ceil(math.log2(sk_ref.shape[1] * tk))))

        def idx_body(it, lo):
            t = lo + jnp.left_shift(jnp.int32(1), n_bits - 1 - it)
            return jnp.where(count_eq_below(t) < need, t, lo)

        last = lax.fori_loop(0, n_bits, idx_body, jnp.zeros((1, tq), jnp.int32))

        def drop_body(i, carry):
            idx = i * tk + lax.broadcasted_iota(jnp.int32, (tk, tq), 0)
            k = keys_ref[i]
            drop = jnp.logical_and(jnp.logical_and(k == thr, idx > last), tie)
            keys_ref[i] = jnp.where(drop, INT_MIN, k)
            return carry

        lax.fori_loop(0, n_tiles, drop_body, 0)

    thr_eff = jnp.maximum(thr, INT_MIN + 1)

    m_ref[...] = jnp.full(m_ref.shape, M_INIT, F32)
    acc_ref[...] = jnp.zeros(acc_ref.shape, F32)
    p_row = prow_ref[0]

    def attend(linear_alibi):
        def score(i, slot, h):
            k_t = sk_ref[0, i] if linear_alibi else sk_ref[0, i][:, 0:HEAD_DIM]
            q_t = sqt_ref[0, h] if linear_alibi else sqt_ref[0, h][0:HEAD_DIM, :]
            s_ref[slot, h] = jnp.dot(k_t, q_t, preferred_element_type=F32)

        def bias(i):
            sel = keys_ref[i] >= thr_eff
            if linear_alibi:
                return jnp.where(sel, 0.0, MASKED)
            return jnp.where(sel, jnp.abs(p_row - pcol_ref[0, i]).astype(F32), FAR)

        def consume(i, slot, diag, h, b):
            del diag
            s = s_ref[slot, h] + b if linear_alibi else s_ref[slot, h] - (slopes[h] * LOG2E) * b
            _softmax_step(s, svt_ref[0, i], m_ref, acc_ref, h)

        _pipelined_tiles(j, score, consume, prepare=bias, lookahead=ATT_LOOKAHEAD)

    linear = flag_ref[pl.program_id(0)] != 0

    @pl.when(linear)
    def _():
        attend(True)

    @pl.when(jnp.logical_not(linear))
    def _():
        attend(False)

    for h in range(N_HEADS):
        o_ref[0, h] = _normalized(acc_ref[h]).astype(o_ref.dtype)


def _dsa(linear_alibi, sqt, sk, svt, iqt, ik, iw_t, pos_row, pos_col, topk, slopes):
    b, h, dq, s = sqt.shape
    nk, tk = sk.shape[1], sk.shape[2]
    d = HEAD_DIM
    tq = ATT_TQ
    n_key_tiles = nk + COUNT_GROUP - 1
    kv = lambda a: pl.BlockSpec((1,) + a.shape[1:], lambda bi, j: (bi,) + (0,) * (a.ndim - 1))
    return pl.pallas_call(
        functools.partial(_dsa_kernel, topk=topk, slopes=slopes),
        grid=(b, s // tq),
        in_specs=[pl.BlockSpec(memory_space=pltpu.SMEM),
                  pl.BlockSpec((1, h, dq, tq), lambda bi, j: (bi, 0, 0, j)),
                  kv(sk), kv(svt),
                  pl.BlockSpec((1, h, IDX_DIM, tq), lambda bi, j: (bi, 0, 0, j)),
                  kv(ik),
                  pl.BlockSpec((1, h, tq), lambda bi, j: (bi, 0, j)),
                  pl.BlockSpec((1, 1, tq), lambda bi, j: (bi, 0, j)),
                  kv(pos_col)],
        out_specs=pl.BlockSpec((1, h, d, tq), lambda bi, j: (bi, 0, 0, j)),
        out_shape=jax.ShapeDtypeStruct((b, h, d, s), BF16),
        scratch_shapes=[pltpu.VMEM((n_key_tiles, tk, tq), jnp.int32),
                        pltpu.VMEM((n_key_tiles, tk, tq), jnp.int16), pltpu.VMEM((n_key_tiles, tk, tq), jnp.int16),
                        pltpu.VMEM((2, h, tk, tq), F32),
                        pltpu.VMEM((h, 1, tq), F32), pltpu.VMEM((h, svt.shape[2], tq), F32)],
        compiler_params=_cparams(("parallel", "arbitrary")),
        name="dsa",
    )(linear_alibi, sqt, sk, svt, iqt, ik, iw_t, pos_row, pos_col)


def _layernorm(r, g, b):
    mu = jnp.mean(r, -1, keepdims=True)
    d = r - mu
    var = jnp.mean(jnp.square(d), -1, keepdims=True)
    return d * lax.rsqrt(var + LN_EPS) * g + b


def _merge_kernel(o1_ref, o2_ref, o3_ref, g_ref, x_ref, wb_ref, wo_ref, lng_ref, lnb_ref, xo_ref, xb_ref, *, alpha):
    tm, dm = x_ref.shape[1], x_ref.shape[2]
    merged = jnp.zeros((dm, tm), F32)
    for n, o_ref in enumerate((o1_ref, o2_ref, o3_ref)):
        o_t = o_ref[0].reshape(N_HEADS * HEAD_DIM, tm)
        proj = jnp.dot(wb_ref[n], o_t, preferred_element_type=F32)
        merged = merged + g_ref[0, n * dm:(n + 1) * dm, :].astype(F32) * proj
    y = jnp.dot(merged.T.astype(BF16), wo_ref[...], preferred_element_type=F32)
    out = _layernorm(alpha * x_ref[0] + y, lng_ref[...], lnb_ref[...])
    xo_ref[0] = out
    xb_ref[0] = out.astype(BF16)


def _merge(o1, o2, o3, gates, x, wb_t, wo, lng, lnb, alpha):
    bsz, seq, dm = x.shape
    tm = ATT_TQ
    o_spec = pl.BlockSpec((1, N_HEADS, HEAD_DIM, tm), lambda b, i: (b, 0, 0, i))
    blk = pl.BlockSpec((1, tm, dm), lambda b, i: (b, i, 0))
    return pl.pallas_call(
        functools.partial(_merge_kernel, alpha=alpha),
        grid=(bsz, seq // tm),
        in_specs=[o_spec, o_spec, o_spec, pl.BlockSpec((1, N_BRANCH * dm, tm), lambda b, i: (b, 0, i)), blk,
                  _resident(wb_t), _resident(wo), _resident(lng), _resident(lnb)],
        out_specs=[blk, blk],
        out_shape=[jax.ShapeDtypeStruct((bsz, seq, dm), F32), jax.ShapeDtypeStruct((bsz, seq, dm), BF16)],
        compiler_params=_cparams(("parallel", "parallel")),
        name="merge",
    )(o1, o2, o3, gates, x, wb_t, wo, lng, lnb)


def _ffn_kernel(x_ref, wg_ref, wv_ref, cw_ref, cb_ref, wd_ref, lng_ref, lnb_ref, xo_ref, xb_ref,
                tail_ref, gv_ref, y_ref, *, alpha):
    tm = x_ref.shape[1]
    n_chunks = wg_ref.shape[0]
    x = x_ref[0]
    xb = x.astype(BF16)
    row = lax.broadcasted_iota(jnp.int32, (tm, FF_CHUNK), 0)
    y_ref[...] = jnp.zeros(y_ref.shape, F32)

    @pl.when(pl.program_id(1) == 0)
    def _():
        tail_ref[...] = jnp.zeros(tail_ref.shape, F32)

    def up(c, slot):
        gv_ref[slot, 0] = jnp.dot(xb, wg_ref[c], preferred_element_type=F32)
        gv_ref[slot, 1] = jnp.dot(xb, wv_ref[c], preferred_element_type=F32)

    def down(c, slot):
        g = gv_ref[slot, 0]
        tail = tail_ref[c]
        g1 = jnp.where(row < 1, tail[7:8, :], pltpu.roll(g, 1, axis=0))
        g2 = jnp.where(row < 2, jnp.where(row < 1, tail[6:7, :], tail[7:8, :]), pltpu.roll(g, 2, axis=0))
        cw = cw_ref[c]
        conv = cb_ref[c] + g2 * cw[0:1, :] + g1 * cw[1:2, :] + g * cw[2:3, :]
        gelu = 0.5 * conv * (1.0 + lax.erf(conv * (2.0 ** -0.5)))
        hid = (gelu * gv_ref[slot, 1]).astype(BF16)
        y_ref[...] += jnp.dot(hid, wd_ref[c], preferred_element_type=F32)
        tail_ref[c] = g[tm - 8:tm, :]

    up(0, 0)

    def pair(p, carry):
        c = 2 * p
        up(c + 1, 1)
        down(c, 0)
        up(c + 2, 0)
        down(c + 1, 1)
        return carry

    lax.fori_loop(0, (n_chunks - 1) // 2, pair, 0)
    if n_chunks % 2 == 0:
        up(n_chunks - 1, 1)
        down(n_chunks - 2, 0)
        down(n_chunks - 1, 1)
    else:
        down(n_chunks - 1, 0)
    out = _layernorm(alpha * x + y_ref[...], lng_ref[...], lnb_ref[...])
    xo_ref[0] = out
    xb_ref[0] = out.astype(BF16)


def _ffn(x, wg, wv, cw, cb, wd, lng, lnb, alpha, tm):
    b, s, dm = x.shape
    n_chunks = wg.shape[0]
    blk = pl.BlockSpec((1, tm, dm), lambda bi, i: (bi, i, 0))
    return pl.pallas_call(
        functools.partial(_ffn_kernel, alpha=alpha),
        grid=(b, s // tm),
        in_specs=[blk] + [_resident(a) for a in (wg, wv, cw, cb, wd, lng, lnb)],
        out_specs=[blk, blk],
        out_shape=[jax.ShapeDtypeStruct((b, s, dm), F32), jax.ShapeDtypeStruct((b, s, dm), BF16)],
        scratch_shapes=[pltpu.VMEM((n_chunks, 8, FF_CHUNK), F32), pltpu.VMEM((2, 2, tm, FF_CHUNK), F32),
                        pltpu.VMEM((tm, dm), F32)],
        compiler_params=_cparams(("parallel", "arbitrary")),
        name="ffn",
    )(x, wg, wv, cw, cb, wd, lng, lnb)


def _split_offsets(d_model):
    sizes = (MLA_Q_RANK, MLA_KV_RANK, MLA_ROPE, 512, 512, 512, N_HEADS, 512, HEAD_DIM, HEAD_DIM,
             N_HEADS * IDX_DIM, IDX_DIM, N_HEADS, N_BRANCH * d_model)
    offs, o = [], 0
    for w in sizes:
        offs.append((o, o + w))
        o += w
    return offs


def _proj_weights(w_in, q_norm, kv_norm, w_uq, w_ukv):
    d = w_in.shape[0]
    (c_q, c_kv, k_r, f_q, f_k, f_v, f_l, s_q, s_k, s_v, i_q, i_k, i_w, gate) = [
        w_in[:, a:b] for a, b in _split_offsets(d)]
    half = MLA_ROPE // 2
    k_r_rot = jnp.concatenate([-k_r[:, half:], k_r[:, :half]], axis=1)
    pad_cols = lambda a, n: jnp.pad(a, ((0, 0), (0, n - a.shape[1])))
    f_k_heads = jnp.pad(f_k.reshape(d, N_HEADS, HEAD_DIM), ((0, 0), (0, 0), (0, LANES - HEAD_DIM))).reshape(d, -1)
    w_row = jnp.concatenate([c_kv, c_q, k_r, k_r_rot, pad_cols(f_l, LANES), f_k_heads,
                             pad_cols(s_k, LANES), pad_cols(i_k, LANES)], axis=1).astype(BF16)
    w_col = jnp.concatenate([f_q, f_v, s_q, s_v, i_q, pad_cols(i_w, 16), pad_cols(f_l, 16), gate], axis=1)
    w_col_t = w_col.T.astype(BF16)

    r = w_uq.shape[0]
    wq = w_uq.reshape(r, N_HEADS, MLA_QK)
    rope = wq[:, :, MLA_NOPE:]
    wq_rot = jnp.concatenate([jnp.zeros((r, N_HEADS, MLA_NOPE), w_uq.dtype), -rope[:, :, half:], rope[:, :, :half]], axis=2)
    wkv = w_ukv.reshape(w_ukv.shape[0], N_HEADS, MLA_NOPE + HEAD_DIM)
    wk = jnp.pad(wkv[:, :, :MLA_NOPE], ((0, 0), (0, 0), (0, LANES - MLA_NOPE))).reshape(wkv.shape[0], -1)
    wv = wkv[:, :, MLA_NOPE:].reshape(wkv.shape[0], -1)
    place = jnp.zeros((MLA_ROPE, N_HEADS, LANES), F32)
    place = place.at[jnp.arange(MLA_ROPE), :, MLA_NOPE + jnp.arange(MLA_ROPE)].set(1.0).reshape(MLA_ROPE, -1)
    return (w_row, w_col_t, q_norm.reshape(1, -1), kv_norm.reshape(1, -1),
            w_uq.T.astype(BF16), wq_rot.reshape(r, -1).T.astype(BF16), wk.astype(BF16), wv.T.astype(BF16),
            place.astype(BF16))


def _fox_placement():
    h = jnp.arange(N_HEADS)
    pk = jnp.zeros((3, FOX_PAD, N_HEADS * LANES), F32)
    pq = jnp.zeros((3, N_HEADS * FOX_PAD, FOX_PAD), F32)
    one_k = jnp.zeros((1, N_HEADS * LANES), F32)
    one_q = jnp.zeros((N_HEADS * FOX_PAD, 1), F32)
    for n in range(3):
        pk = pk.at[n, h, LANES * h + HEAD_DIM + 3 + n].set(-1.0)
        pq = pq.at[n, FOX_PAD * h + n, h].set(1.0)
        one_k = one_k.at[0, LANES * h + HEAD_DIM + n].set(1.0)
        one_q = one_q.at[FOX_PAD * h + 3 + n, 0].set(1.0)
    return pk.astype(BF16), one_k, pq.astype(BF16), one_q


def _alibi_placement(slopes):
    a = jnp.asarray(slopes, F32) * LOG2E
    neg_a = jnp.pad(-a, (0, FOX_PAD - N_HEADS)).reshape(FOX_PAD, 1)
    terms = jnp.stack([t.astype(F32) for t in _split3(a)], axis=1)
    a_q = jnp.zeros((N_HEADS, FOX_PAD), F32).at[:, 3:6].set(terms).at[:, 6:9].set(terms).reshape(-1, 1)
    e_k = jnp.zeros((3, LANES), F32)
    for n in range(3):
        e_k = e_k.at[n, HEAD_DIM + 3 * n:HEAD_DIM + 3 * n + 3].set(1.0)
    return neg_a, a_q, e_k


def _rope_tables(positions):
    half = MLA_ROPE // 2
    inv = ROPE_THETA ** (-jnp.arange(half, dtype=F32) / half)
    ang = positions.astype(F32)[..., None] * inv
    cos, sin = jnp.cos(ang), jnp.sin(ang)
    b, s = positions.shape
    q_scale = (MLA_QK ** -0.5) * LOG2E
    cos2 = jnp.concatenate([cos, cos], -1)
    sin2 = jnp.concatenate([sin, sin], -1)
    ones = jnp.ones((b, s, MLA_NOPE), F32)
    cq = jnp.tile(jnp.concatenate([ones, cos2], -1) * q_scale, (1, 1, N_HEADS))
    sq = jnp.tile(jnp.concatenate([0.0 * ones, sin2], -1) * q_scale, (1, 1, N_HEADS))
    return cq.transpose(0, 2, 1), sq.transpose(0, 2, 1), cos2, sin2


def kernel(x, positions, w_in, b_gate, b_forget, mla_q_norm, mla_kv_norm, mla_w_uq, mla_w_ukv, w_branch, w_out,
           ln1_g, ln1_b, ffn_w_up, ffn_conv_w, ffn_conv_b, ffn_w_down, ln2_g, ln2_b):
    bsz, seq, dm = x.shape
    depth = w_in.shape[0]
    d_ff = ffn_w_down.shape[1]
    tk = ATT_TK
    nk = seq // tk
    topk = min(DSA_MAX_TOPK, seq // 4)
    alpha = (2 * depth) ** 0.25
    slopes = tuple(2.0 ** (-8.0 * i / N_HEADS) for i in range(1, N_HEADS + 1))
    n_chunks = d_ff // FF_CHUNK
    chunks = lambda w: w.reshape(w.shape[0], n_chunks, FF_CHUNK).transpose(1, 0, 2)

    pos_row = positions.reshape(bsz, 1, seq)
    pos_col = positions.reshape(bsz, nk, tk, 1)
    tabs = _rope_tables(positions) + (pos_row.astype(F32), positions.reshape(bsz, seq, 1).astype(F32))
    linear_alibi = (jnp.all(positions[:, 1:] >= positions[:, :-1], axis=1)
                    & (jnp.min(positions, axis=1) >= 0) & (jnp.max(positions, axis=1) < POS_LIMIT)).astype(jnp.int32)

    xf = x
    xb = x.astype(BF16)
    for l in range(depth):
        (mq, mk, mv, fq, fk, fv, sq, sk, sv, iq, ik, iw, gates) = _proj(
            xb, tabs, _proj_weights(w_in[l], mla_q_norm[l], mla_kv_norm[l], mla_w_uq[l], mla_w_ukv[l]),
            b_forget[l], b_gate[l], slopes)
        o_mla = _attention(mq, mk, mv)
        o_fox = _attention(fq, fk, fv)
        o_dsa = _dsa(linear_alibi, sq, sk, sv, iq, ik, iw, pos_row, pos_col, topk, slopes)
        xf, xb = _merge(o_mla, o_fox, o_dsa, gates, xf,
                        w_branch[l].transpose(0, 2, 1).astype(BF16), w_out[l].astype(BF16),
                        ln1_g[l].reshape(1, -1), ln1_b[l].reshape(1, -1), alpha)
        w_up = ffn_w_up[l]
        xf, xb = _ffn(xf, chunks(w_up[:, :d_ff]).astype(BF16), chunks(w_up[:, d_ff:]).astype(BF16),
                      chunks(ffn_conv_w[l]), chunks(ffn_conv_b[l].reshape(1, -1)),
                      ffn_w_down[l].reshape(n_chunks, FF_CHUNK, dm).astype(BF16),
                      ln2_g[l].reshape(1, -1), ln2_b[l].reshape(1, -1), alpha, min(512, seq))
    return xf
```

```python
import functools
import math

import jax
import jax.numpy as jnp
from jax import lax
from jax.experimental import pallas as pl
from jax.experimental.pallas import tpu as pltpu

F32 = jnp.float32
BF16 = jnp.bfloat16

N_HEADS = 8
HEAD_DIM = 64
MLA_Q_RANK = 192
MLA_KV_RANK = 128
MLA_NOPE = 64
MLA_ROPE = 32
MLA_QK = MLA_NOPE + MLA_ROPE
IDX_DIM = 32
N_BRANCH = 3
CONV_WIDTH = 3
ROPE_THETA = 10000.0
DSA_MAX_TOPK = 256
LN_EPS = 1e-5
RMS_EPS = 1e-6
LOG2E = math.log2(math.e)
LANES = 128

ZR_CKV, ZR_CQ, ZR_KR, ZR_KRR = 0, 128, 320, 352
ZR_FL = 384
ZR_FK = 512
ZR_SK = 1536
ZR_IK = 1664
ZR_W = 1792
ZC_FQ, ZC_FV, ZC_SQ, ZC_SV, ZC_IQ, ZC_IW, ZC_FL, ZC_G = 0, 512, 1024, 1536, 1600, 1856, 1872, 1888
FOX_QK = 80
FOX_PAD = 16
DSA_QK = 80
ATT_V = 80
ATT_LOOKAHEAD = 3
POS_LIMIT = 1 << 16

ATT_TQ = 256
ATT_TK = 256
MASKED = -1e30
M_INIT = -5e29
FAR = 1e33
INT_MIN = -2 ** 31
COUNT_GROUP = 4
FF_CHUNK = 256
FFN_TM = 1024
MERGE_TM = 512
VMEM_LIMIT = 56 * 1024 * 1024


def _cparams(sem):
    return pltpu.CompilerParams(dimension_semantics=sem, vmem_limit_bytes=VMEM_LIMIT)


def _resident(a):
    return pl.BlockSpec(a.shape, lambda *_: (0,) * a.ndim, pipeline_mode=pl.Buffered(1))


def _nt(w, x):
    return lax.dot_general(w, x, (((1,), (1,)), ((), ())), preferred_element_type=F32)


def _split3(c):
    hi = c.astype(BF16)
    r1 = c - hi.astype(F32)
    mid = r1.astype(BF16)
    lo = (r1 - mid.astype(F32)).astype(BF16)
    return hi, mid, lo


def _proj_kernel(x_ref, cq_ref, sq_ref, ck_ref, sk_ref, prow_ref, pcol_ref, wr_ref, wc_ref, qn_ref, kvn_ref,
                 wq_ref, wqr_ref, wk_ref, wv_ref, e_ref, bfr_ref, bfc_ref, tril_ref, triu_ref, pk_ref, onek_ref,
                 pq_ref, oneq_ref, slope_ref, aq_ref, ek_ref, bg_ref,
                 mq_o, mk_o, mv_o, fq_o, fk_o, fv_o, sq_o, sk_o, sv_o, iq_o, ik_o, iw_o, g_o, cr_ref, cc_ref):
    tm = x_ref.shape[1]

    @pl.when(pl.program_id(1) == 0)
    def _():
        cr_ref[...] = jnp.zeros(cr_ref.shape, F32)
        cc_ref[...] = jnp.zeros(cc_ref.shape, F32)

    x = x_ref[0]
    zr = jnp.dot(x, wr_ref[...], preferred_element_type=F32)
    zc = _nt(wc_ref[0:ZC_G, :], x)

    c_kv = zr[:, ZR_CKV:ZR_CKV + MLA_KV_RANK]
    c_q = zr[:, ZR_CQ:ZR_CQ + MLA_Q_RANK]
    nq = (c_q * lax.rsqrt(jnp.mean(jnp.square(c_q), -1, keepdims=True) + RMS_EPS) * qn_ref[...]).astype(BF16)
    nkv = (c_kv * lax.rsqrt(jnp.mean(jnp.square(c_kv), -1, keepdims=True) + RMS_EPS) * kvn_ref[...]).astype(BF16)
    q_m = _nt(wq_ref[...], nq) * cq_ref[0] + _nt(wqr_ref[...], nq) * sq_ref[0]
    k_rope = (zr[:, ZR_KR:ZR_KR + MLA_ROPE] * ck_ref[0] + zr[:, ZR_KRR:ZR_KRR + MLA_ROPE] * sk_ref[0]).astype(BF16)
    k_m = (jnp.dot(nkv, wk_ref[...], preferred_element_type=F32)
           + jnp.dot(k_rope, e_ref[...], preferred_element_type=F32))
    v_m = _nt(wv_ref[...], nkv)

    lf_r = jax.nn.log_sigmoid(zr[:, ZR_FL:ZR_FL + FOX_PAD] + bfr_ref[...])
    cum_r = jnp.dot(tril_ref[...], lf_r, preferred_element_type=F32, precision=lax.Precision.HIGHEST) + cr_ref[...]
    lf_c = jax.nn.log_sigmoid(zc[ZC_FL:ZC_FL + FOX_PAD] + bfc_ref[...])
    cum_c = jnp.dot(lf_c, triu_ref[...], preferred_element_type=F32, precision=lax.Precision.HIGHEST) + cc_ref[...]
    cr_ref[...] = cum_r[tm - 1:tm, :]
    cc_ref[...] = cum_c[:, tm - 1:tm]
    k_terms = jnp.concatenate(_split3(cum_r * LOG2E), axis=1)
    k_bias = onek_ref[...] + jnp.dot(k_terms, pk_ref[...], preferred_element_type=F32)
    q_bias = oneq_ref[...]
    for n, term in enumerate(_split3(cum_c * LOG2E)):
        q_bias = q_bias + jnp.dot(pq_ref[n], term, preferred_element_type=F32)
    f_k = zr[:, ZR_FK:ZR_FK + N_HEADS * LANES] + k_bias
    q_scale = HEAD_DIM ** -0.5 * LOG2E
    f_q = zc[ZC_FQ:ZC_FQ + N_HEADS * HEAD_DIM] * q_scale
    s_q = zc[ZC_SQ:ZC_SQ + N_HEADS * HEAD_DIM] * q_scale

    d_bias = aq_ref[...]
    for n, term in enumerate(_split3(slope_ref[...] * prow_ref[0])):
        d_bias = d_bias + jnp.dot(pq_ref[n], term, preferred_element_type=F32)
    p_k = pcol_ref[0]
    p_hi = p_k.astype(BF16).astype(F32)
    s_k = (zr[:, ZR_SK:ZR_SK + LANES] + ek_ref[0:1, :] + p_hi * ek_ref[1:2, :] + (p_k - p_hi) * ek_ref[2:3, :])

    ones_row = lax.broadcasted_iota(jnp.int32, (ATT_V - HEAD_DIM, tm), 0) == 0
    ones_blk = jnp.where(ones_row, 1.0, 0.0).astype(BF16)
    for h in range(N_HEADS):
        mq_o[0, h] = q_m[MLA_QK * h:MLA_QK * (h + 1)].astype(BF16)
        mk_o[0, h, 0] = k_m[:, LANES * h:LANES * h + MLA_QK].astype(BF16)
        mv_o[0, h, 0, 0:HEAD_DIM, :] = v_m[HEAD_DIM * h:HEAD_DIM * (h + 1)].astype(BF16)
        mv_o[0, h, 0, HEAD_DIM:ATT_V, :] = ones_blk
        fq_o[0, h, 0:HEAD_DIM, :] = f_q[HEAD_DIM * h:HEAD_DIM * (h + 1)].astype(BF16)
        fq_o[0, h, HEAD_DIM:FOX_QK, :] = q_bias[FOX_PAD * h:FOX_PAD * (h + 1)].astype(BF16)
        fk_o[0, h, 0] = f_k[:, LANES * h:LANES * h + FOX_QK].astype(BF16)
        fv_o[0, h, 0, 0:HEAD_DIM, :] = zc[ZC_FV + HEAD_DIM * h:ZC_FV + HEAD_DIM * (h + 1)].astype(BF16)
        fv_o[0, h, 0, HEAD_DIM:ATT_V, :] = ones_blk
        sq_o[0, h, 0:HEAD_DIM, :] = s_q[HEAD_DIM * h:HEAD_DIM * (h + 1)].astype(BF16)
        sq_o[0, h, HEAD_DIM:DSA_QK, :] = d_bias[FOX_PAD * h:FOX_PAD * (h + 1)].astype(BF16)
        iq_o[0, h] = zc[ZC_IQ + IDX_DIM * h:ZC_IQ + IDX_DIM * (h + 1)].astype(BF16)

    sk_o[0, 0] = s_k[:, 0:DSA_QK].astype(BF16)
    sv_o[0, 0, 0:HEAD_DIM, :] = zc[ZC_SV:ZC_SV + HEAD_DIM].astype(BF16)
    sv_o[0, 0, HEAD_DIM:ATT_V, :] = ones_blk
    ik_o[0, 0] = zr[:, ZR_IK:ZR_IK + IDX_DIM].astype(BF16)
    iw_o[0] = zc[ZC_IW:ZC_IW + N_HEADS]

    g_o[0] = jax.nn.sigmoid(_nt(wc_ref[ZC_G:, :], x) + bg_ref[...]).astype(BF16)


def _proj(x, tabs, w, b_forget, b_gate, slopes):
    bsz, seq, dm = x.shape
    tm = ATT_TK
    nk = seq // tm
    h = N_HEADS
    cq_t, sq_t, ck, sk, pos_row, pos_col = tabs
    n_gate = N_BRANCH * dm
    bf16 = lambda *s: jax.ShapeDtypeStruct(s, BF16)
    feat = lambda rows: pl.BlockSpec((1, rows, tm), lambda b, i: (b, 0, i))
    head_t = lambda d: pl.BlockSpec((1, h, d, tm), lambda b, i: (b, 0, 0, i))
    head_k = lambda d: pl.BlockSpec((1, h, 1, tm, d), lambda b, i: (b, 0, i, 0, 0))
    head_vt = pl.BlockSpec((1, h, 1, ATT_V, tm), lambda b, i: (b, 0, i, 0, 0))
    one_k = lambda d: pl.BlockSpec((1, 1, tm, d), lambda b, i: (b, i, 0, 0))
    one_vt = pl.BlockSpec((1, 1, ATT_V, tm), lambda b, i: (b, i, 0, 0))
    tok = lambda d: pl.BlockSpec((1, tm, d), lambda b, i: (b, i, 0))
    b_f = jnp.pad(b_forget, (0, FOX_PAD - h))
    consts = list(w) + [
        b_f.reshape(1, FOX_PAD), b_f.reshape(FOX_PAD, 1),
        (jnp.arange(tm)[:, None] >= jnp.arange(tm)[None, :]).astype(F32),
        (jnp.arange(tm)[:, None] <= jnp.arange(tm)[None, :]).astype(F32),
    ] + list(_fox_placement()) + list(_alibi_placement(slopes)) + [b_gate.reshape(n_gate, 1)]
    return pl.pallas_call(
        _proj_kernel,
        grid=(bsz, nk),
        in_specs=[tok(dm), feat(h * MLA_QK), feat(h * MLA_QK), tok(MLA_ROPE), tok(MLA_ROPE), feat(1), tok(1)]
                 + [_resident(c) for c in consts],
        out_specs=[head_t(MLA_QK), head_k(MLA_QK), head_vt, head_t(FOX_QK), head_k(FOX_QK), head_vt,
                   head_t(DSA_QK), one_k(DSA_QK), one_vt, head_t(IDX_DIM), one_k(IDX_DIM),
                   feat(h), feat(n_gate)],
        out_shape=[bf16(bsz, h, MLA_QK, seq), bf16(bsz, h, nk, tm, MLA_QK), bf16(bsz, h, nk, ATT_V, tm),
                   bf16(bsz, h, FOX_QK, seq), bf16(bsz, h, nk, tm, FOX_QK), bf16(bsz, h, nk, ATT_V, tm),
                   bf16(bsz, h, DSA_QK, seq), bf16(bsz, nk, tm, DSA_QK), bf16(bsz, nk, ATT_V, tm),
                   bf16(bsz, h, IDX_DIM, seq), bf16(bsz, nk, tm, IDX_DIM),
                   jax.ShapeDtypeStruct((bsz, h, seq), F32), bf16(bsz, n_gate, seq)],
        scratch_shapes=[pltpu.VMEM((1, FOX_PAD), F32), pltpu.VMEM((FOX_PAD, 1), F32)],
        compiler_params=_cparams(("parallel", "arbitrary")),
        name="proj",
    )(x, cq_t, sq_t, ck, sk, pos_row, pos_col, *consts)


def _softmax_step(s, v_t, m_ref, acc_ref, h):
    m_old = m_ref[h]
    m_new = jnp.maximum(m_old, jnp.max(s, axis=0, keepdims=True))
    p = jnp.exp2(s - m_new)
    acc_ref[h] = jnp.exp2(m_old - m_new) * acc_ref[h] + jnp.dot(v_t, p.astype(BF16), preferred_element_type=F32)
    m_ref[h] = m_new


def _normalized(acc):
    return acc[0:HEAD_DIM] / acc[HEAD_DIM:HEAD_DIM + 1]


def _causal_mask(tk, tq):
    return lax.broadcasted_iota(jnp.int32, (tk, tq), 0) <= lax.broadcasted_iota(jnp.int32, (tk, tq), 1)


def _pipelined_tiles(j, matmuls, consume, prepare=None, lookahead=N_HEADS):
    def run(cur, slot, diag, nxt=None):
        shared = prepare(cur) if prepare is not None else None
        if nxt is not None:
            for h in range(lookahead):
                matmuls(nxt, 1 - slot, h)
        for h in range(N_HEADS):
            consume(cur, slot, diag, h, shared)
            if nxt is not None and h + lookahead < N_HEADS:
                matmuls(nxt, 1 - slot, h + lookahead)

    for h in range(N_HEADS):
        matmuls(0, 0, h)

    def pair(p, carry):
        t = 2 * p
        run(t, 0, False, t + 1)
        run(t + 1, 1, False, t + 2)
        return carry

    lax.fori_loop(0, j // 2, pair, 0)
    odd = j % 2 == 1

    @pl.when(odd)
    def _():
        run(j - 1, 0, False, j)
        run(j, 1, True)

    @pl.when(jnp.logical_not(odd))
    def _():
        run(j, 0, True)


def _attn_kernel(qt_ref, k_ref, vt_ref, o_ref, s_ref, m_ref, acc_ref):
    j = pl.program_id(1)
    tk, tq = k_ref.shape[3], qt_ref.shape[3]
    m_ref[...] = jnp.full(m_ref.shape, M_INIT, F32)
    acc_ref[...] = jnp.zeros(acc_ref.shape, F32)

    def score(i, slot, h):
        s_ref[slot, h] = jnp.dot(k_ref[0, h, i], qt_ref[0, h], preferred_element_type=F32)

    def consume(i, slot, diag, h, shared):
        s = s_ref[slot, h]
        if diag:
            s = jnp.where(_causal_mask(tk, tq), s, MASKED)
        _softmax_step(s, vt_ref[0, h, i], m_ref, acc_ref, h)

    _pipelined_tiles(j, score, consume, lookahead=ATT_LOOKAHEAD)
    for h in range(N_HEADS):
        o_ref[0, h] = _normalized(acc_ref[h]).astype(o_ref.dtype)


def _attention(qt, k, vt):
    b, h, dk, s = qt.shape
    nk, tk = k.shape[2], k.shape[3]
    tq = ATT_TQ
    return pl.pallas_call(
        _attn_kernel,
        grid=(b, s // tq),
        in_specs=[pl.BlockSpec((1, h, dk, tq), lambda bi, j: (bi, 0, 0, j)),
                  pl.BlockSpec((1, h, nk, tk, dk), lambda bi, j: (bi, 0, 0, 0, 0)),
                  pl.BlockSpec((1, h, nk, ATT_V, tk), lambda bi, j: (bi, 0, 0, 0, 0))],
        out_specs=pl.BlockSpec((1, h, HEAD_DIM, tq), lambda bi, j: (bi, 0, 0, j)),
        out_shape=jax.ShapeDtypeStruct((b, h, HEAD_DIM, s), BF16),
        scratch_shapes=[pltpu.VMEM((2, h, tk, tq), F32),
                        pltpu.VMEM((h, 1, tq), F32), pltpu.VMEM((h, ATT_V, tq), F32)],
        compiler_params=_cparams(("parallel", "arbitrary")),
        name="flash_attn",
    )(qt, k, vt)


def _sortable_key(x):
    bits = lax.bitcast_convert_type(x, jnp.int32)
    return jnp.where(bits == INT_MIN, 0, bits ^ ((bits >> 31) & 0x7FFFFFFF))


def _count_ge(keys_ref, n_tiles, cand):
    tk, tq = keys_ref.shape[1], keys_ref.shape[2]

    def body(i, c):
        hit = keys_ref[i] >= cand
        return c + jnp.sum(hit.astype(jnp.int32).reshape(tk // 8, 8, tq), axis=0)

    c = lax.fori_loop(0, n_tiles, body, jnp.zeros((8, tq), jnp.int32))
    return jnp.sum(c, axis=0, keepdims=True)


def _count16(ref, n_groups, cand, strict=False):
    tk, tq = ref.shape[1], ref.shape[2]
    c16 = cand.astype(jnp.int16)

    def body(p, c):
        for t in range(COUNT_GROUP):
            k = ref[COUNT_GROUP * p + t]
            hit = (k > c16) if strict else (k >= c16)
            one = jnp.where(hit, jnp.int16(1), jnp.int16(0))
            for r in range(tk // 16):
                c = c + one[16 * r:16 * (r + 1), :]
        return c

    c = lax.fori_loop(0, n_groups, body, jnp.zeros((16, tq), jnp.int16))
    return jnp.sum(c.astype(jnp.int32), axis=0, keepdims=True)


def _kth_largest16(ref, n_groups, kth):
    tq = ref.shape[2]
    cnt = _count16(ref, n_groups, jnp.zeros((1, tq), jnp.int32))
    prefix = jnp.where(cnt >= kth, jnp.int32(0), jnp.int32(-32768))

    def bit_body(it, prefix):
        cand = prefix + jnp.left_shift(jnp.int32(1), 14 - it)
        return jnp.where(_count16(ref, n_groups, cand) >= kth, cand, prefix)

    return lax.fori_loop(0, 15, bit_body, prefix)


def _dsa_kernel(flag_ref, sqt_ref, sk_ref, svt_ref, iqt_ref, ik_ref, iw_ref, prow_ref, pcol_ref, o_ref,
                keys_ref, khi_ref, klo_ref, s_ref, m_ref, acc_ref, *, topk, slopes):
    j = pl.program_id(1)
    tk, tq = sk_ref.shape[2], sqt_ref.shape[3]
    n_tiles = j + 1
    n_groups = (n_tiles + COUNT_GROUP - 1) // COUNT_GROUP
    causal = _causal_mask(tk, tq)

    def store_keys(i, key):
        keys_ref[i] = key
        khi_ref[i] = (key >> 16).astype(jnp.int16)
        klo_ref[i] = ((key & 0xFFFF) - 32768).astype(jnp.int16)

    def index_logits(i, slot, h):
        s_ref[slot, h] = jnp.dot(ik_ref[0, i], iqt_ref[0, h], preferred_element_type=F32)

    def index_keys(i, slot, diag, h, score):
        term = iw_ref[0, h:h + 1, :] * jnp.maximum(s_ref[slot, h], 0.0)
        score[0] = term if h == 0 else score[0] + term
        if h == N_HEADS - 1:
            key = _sortable_key(score[0])
            if diag:
                key = jnp.where(causal, key, INT_MIN)
            store_keys(i, key)

    _pipelined_tiles(j, index_logits, index_keys, prepare=lambda i: [None])
    for t in range(COUNT_GROUP - 1):
        store_keys(n_tiles + t, jnp.full((tk, tq), INT_MIN, jnp.int32))

    t_hi = _kth_largest16(khi_ref, n_groups, topk)
    need_lo = topk - _count16(khi_ref, n_groups, t_hi, strict=True)
    t_hi16 = t_hi.astype(jnp.int16)

    def class_body(i, carry):
        klo_ref[i] = jnp.where(khi_ref[i] == t_hi16, klo_ref[i], jnp.int16(-32768))
        return carry

    lax.fori_loop(0, n_tiles, class_body, 0)
    t_lo = _kth_largest16(klo_ref, n_groups, need_lo)
    thr = t_hi * 65536 + (t_lo + 32768)

    real = thr > INT_MIN
    cnt_ge = _count_ge(keys_ref, n_tiles, thr)
    tie = jnp.logical_and(real, cnt_ge > topk)
    any_tie = jnp.max(tie.astype(jnp.int32)) > 0

    @pl.when(any_tie)
    def _():
        def count_eq_below(limit):
            def body(i, c):
                idx = i * tk + lax.broadcasted_iota(jnp.int32, (tk, tq), 0)
                hit = jnp.logical_and(keys_ref[i] == thr, idx < limit)
                return c + jnp.sum(hit.astype(jnp.int32).reshape(tk // 8, 8, tq), axis=0)
            c = lax.fori_loop(0, n_tiles, body, jnp.zeros((8, tq), jnp.int32))
            return jnp.sum(c, axis=0, keepdims=True)

        cnt_gt = cnt_ge - count_eq_below(jnp.full((1, tq), n_tiles * tk, jnp.int32))
        need = topk - cnt_gt
        n_bits = max(1, int(math.ceil(math.log2(sk_ref.shape[1] * tk))))

        def idx_body(it, lo):
            t = lo + jnp.left_shift(jnp.int32(1), n_bits - 1 - it)
            return jnp.where(count_eq_below(t) < need, t, lo)

        last = lax.fori_loop(0, n_bits, idx_body, jnp.zeros((1, tq), jnp.int32))

        def drop_body(i, carry):
            idx = i * tk + lax.broadcasted_iota(jnp.int32, (tk, tq), 0)
            k = keys_ref[i]
            drop = jnp.logical_and(jnp.logical_and(k == thr, idx > last), tie)
            keys_ref[i] = jnp.where(drop, INT_MIN, k)
            return carry

        lax.fori_loop(0, n_tiles, drop_body, 0)

    thr_eff = jnp.maximum(thr, INT_MIN + 1)

    m_ref[...] = jnp.full(m_ref.shape, M_INIT, F32)
    acc_ref[...] = jnp.zeros(acc_ref.shape, F32)
    p_row = prow_ref[0]

    def attend(linear_alibi):
        def score(i, slot, h):
            k_t = sk_ref[0, i] if linear_alibi else sk_ref[0, i][:, 0:HEAD_DIM]
            q_t = sqt_ref[0, h] if linear_alibi else sqt_ref[0, h][0:HEAD_DIM, :]
            s_ref[slot, h] = jnp.dot(k_t, q_t, preferred_element_type=F32)

        def bias(i):
            sel = keys_ref[i] >= thr_eff
            if linear_alibi:
                return jnp.where(sel, 0.0, MASKED)
            return jnp.where(sel, jnp.abs(p_row - pcol_ref[0, i]).astype(F32), FAR)

        def consume(i, slot, diag, h, b):
            del diag
            s = s_ref[slot, h] + b if linear_alibi else s_ref[slot, h] - (slopes[h] * LOG2E) * b
            _softmax_step(s, svt_ref[0, i], m_ref, acc_ref, h)

        _pipelined_tiles(j, score, consume, prepare=bias, lookahead=ATT_LOOKAHEAD)

    linear = flag_ref[pl.program_id(0)] != 0

    @pl.when(linear)
    def _():
        attend(True)

    @pl.when(jnp.logical_not(linear))
    def _():
        attend(False)

    for h in range(N_HEADS):
        o_ref[0, h] = _normalized(acc_ref[h]).astype(o_ref.dtype)


def _dsa(linear_alibi, sqt, sk, svt, iqt, ik, iw_t, pos_row, pos_col, topk, slopes):
    b, h, dq, s = sqt.shape
    nk, tk = sk.shape[1], sk.shape[2]
    d = HEAD_DIM
    tq = ATT_TQ
    n_key_tiles = nk + COUNT_GROUP - 1
    kv = lambda a: pl.BlockSpec((1,) + a.shape[1:], lambda bi, j: (bi,) + (0,) * (a.ndim - 1))
    return pl.pallas_call(
        functools.partial(_dsa_kernel, topk=topk, slopes=slopes),
        grid=(b, s // tq),
        in_specs=[pl.BlockSpec(memory_space=pltpu.SMEM),
                  pl.BlockSpec((1, h, dq, tq), lambda bi, j: (bi, 0, 0, j)),
                  kv(sk), kv(svt),
                  pl.BlockSpec((1, h, IDX_DIM, tq), lambda bi, j: (bi, 0, 0, j)),
                  kv(ik),
                  pl.BlockSpec((1, h, tq), lambda bi, j: (bi, 0, j)),
                  pl.BlockSpec((1, 1, tq), lambda bi, j: (bi, 0, j)),
                  kv(pos_col)],
        out_specs=pl.BlockSpec((1, h, d, tq), lambda bi, j: (bi, 0, 0, j)),
        out_shape=jax.ShapeDtypeStruct((b, h, d, s), BF16),
        scratch_shapes=[pltpu.VMEM((n_key_tiles, tk, tq), jnp.int32),
                        pltpu.VMEM((n_key_tiles, tk, tq), jnp.int16), pltpu.VMEM((n_key_tiles, tk, tq), jnp.int16),
                        pltpu.VMEM((2, h, tk, tq), F32),
                        pltpu.VMEM((h, 1, tq), F32), pltpu.VMEM((h, svt.shape[2], tq), F32)],
        compiler_params=_cparams(("parallel", "arbitrary")),
        name="dsa",
    )(linear_alibi, sqt, sk, svt, iqt, ik, iw_t, pos_row, pos_col)


def _layernorm(r, g, b):
    mu = jnp.mean(r, -1, keepdims=True)
    d = r - mu
    var = jnp.mean(jnp.square(d), -1, keepdims=True)
    return d * lax.rsqrt(var + LN_EPS) * g + b


def _merge_kernel(o1_ref, o2_ref, o3_ref, g_ref, x_ref, wb_ref, wo_ref, lng_ref, lnb_ref, xo_ref, xb_ref, *, alpha):
    tm, dm = x_ref.shape[1], x_ref.shape[2]
    merged = jnp.zeros((dm, tm), F32)
    for n, o_ref in enumerate((o1_ref, o2_ref, o3_ref)):
        o_t = o_ref[0].reshape(N_HEADS * HEAD_DIM, tm)
        proj = jnp.dot(wb_ref[n], o_t, preferred_element_type=F32)
        merged = merged + g_ref[0, n * dm:(n + 1) * dm, :].astype(F32) * proj
    y = jnp.dot(merged.T.astype(BF16), wo_ref[...], preferred_element_type=F32)
    out = _layernorm(alpha * x_ref[0] + y, lng_ref[...], lnb_ref[...])
    xo_ref[0] = out
    xb_ref[0] = out.astype(BF16)


def _merge(o1, o2, o3, gates, x, wb_t, wo, lng, lnb, alpha):
    bsz, seq, dm = x.shape
    tm = min(MERGE_TM, seq)
    o_spec = pl.BlockSpec((1, N_HEADS, HEAD_DIM, tm), lambda b, i: (b, 0, 0, i))
    blk = pl.BlockSpec((1, tm, dm), lambda b, i: (b, i, 0))
    return pl.pallas_call(
        functools.partial(_merge_kernel, alpha=alpha),
        grid=(bsz, seq // tm),
        in_specs=[o_spec, o_spec, o_spec, pl.BlockSpec((1, N_BRANCH * dm, tm), lambda b, i: (b, 0, i)), blk,
                  _resident(wb_t), _resident(wo), _resident(lng), _resident(lnb)],
        out_specs=[blk, blk],
        out_shape=[jax.ShapeDtypeStruct((bsz, seq, dm), F32), jax.ShapeDtypeStruct((bsz, seq, dm), BF16)],
        compiler_params=_cparams(("parallel", "parallel")),
        name="merge",
    )(o1, o2, o3, gates, x, wb_t, wo, lng, lnb)


def _ffn_kernel(x_ref, wg_ref, wv_ref, cw_ref, cb_ref, wd_ref, lng_ref, lnb_ref, xo_ref, xb_ref,
                tail_ref, gv_ref, y_ref, *, alpha):
    tm = x_ref.shape[1]
    n_chunks = wg_ref.shape[0]
    x = x_ref[0]
    xb = x.astype(BF16)
    row = lax.broadcasted_iota(jnp.int32, (tm, FF_CHUNK), 0)
    y_ref[...] = jnp.zeros(y_ref.shape, F32)

    @pl.when(pl.program_id(1) == 0)
    def _():
        tail_ref[...] = jnp.zeros(tail_ref.shape, F32)

    def up(c, slot):
        gv_ref[slot, 0] = jnp.dot(xb, wg_ref[c], preferred_element_type=F32)
        gv_ref[slot, 1] = jnp.dot(xb, wv_ref[c], preferred_element_type=F32)

    def down(c, slot):
        g = gv_ref[slot, 0]
        tail = tail_ref[c]
        g1 = jnp.where(row < 1, tail[7:8, :], pltpu.roll(g, 1, axis=0))
        g2 = jnp.where(row < 2, jnp.where(row < 1, tail[6:7, :], tail[7:8, :]), pltpu.roll(g, 2, axis=0))
        cw = cw_ref[c]
        conv = cb_ref[c] + g2 * cw[0:1, :] + g1 * cw[1:2, :] + g * cw[2:3, :]
        gelu = 0.5 * conv * (1.0 + lax.erf(conv * (2.0 ** -0.5)))
        hid = (gelu * gv_ref[slot, 1]).astype(BF16)
        y_ref[...] += jnp.dot(hid, wd_ref[c], preferred_element_type=F32)
        tail_ref[c] = g[tm - 8:tm, :]

    up(0, 0)

    def pair(p, carry):
        c = 2 * p
        up(c + 1, 1)
        down(c, 0)
        up(c + 2, 0)
        down(c + 1, 1)
        return carry

    lax.fori_loop(0, (n_chunks - 1) // 2, pair, 0)
    if n_chunks % 2 == 0:
        up(n_chunks - 1, 1)
        down(n_chunks - 2, 0)
        down(n_chunks - 1, 1)
    else:
        down(n_chunks - 1, 0)
    out = _layernorm(alpha * x + y_ref[...], lng_ref[...], lnb_ref[...])
    xo_ref[0] = out
    xb_ref[0] = out.astype(BF16)


def _ffn(x, wg, wv, cw, cb, wd, lng, lnb, alpha, tm):
    b, s, dm = x.shape
    n_chunks = wg.shape[0]
    blk = pl.BlockSpec((1, tm, dm), lambda bi, i: (bi, i, 0))
    return pl.pallas_call(
        functools.partial(_ffn_kernel, alpha=alpha),
        grid=(b, s // tm),
        in_specs=[blk] + [_resident(a) for a in (wg, wv, cw, cb, wd, lng, lnb)],
        out_specs=[blk, blk],
        out_shape=[jax.ShapeDtypeStruct((b, s, dm), F32), jax.ShapeDtypeStruct((b, s, dm), BF16)],
        scratch_shapes=[pltpu.VMEM((n_chunks, 8, FF_CHUNK), F32), pltpu.VMEM((2, 2, tm, FF_CHUNK), F32),
                        pltpu.VMEM((tm, dm), F32)],
        compiler_params=_cparams(("parallel", "arbitrary")),
        name="ffn",
    )(x, wg, wv, cw, cb, wd, lng, lnb)


def _split_offsets(d_model):
    sizes = (MLA_Q_RANK, MLA_KV_RANK, MLA_ROPE, 512, 512, 512, N_HEADS, 512, HEAD_DIM, HEAD_DIM,
             N_HEADS * IDX_DIM, IDX_DIM, N_HEADS, N_BRANCH * d_model)
    offs, o = [], 0
    for w in sizes:
        offs.append((o, o + w))
        o += w
    return offs


def _proj_weights(w_in, q_norm, kv_norm, w_uq, w_ukv):
    d = w_in.shape[0]
    (c_q, c_kv, k_r, f_q, f_k, f_v, f_l, s_q, s_k, s_v, i_q, i_k, i_w, gate) = [
        w_in[:, a:b] for a, b in _split_offsets(d)]
    half = MLA_ROPE // 2
    k_r_rot = jnp.concatenate([-k_r[:, half:], k_r[:, :half]], axis=1)
    pad_cols = lambda a, n: jnp.pad(a, ((0, 0), (0, n - a.shape[1])))
    f_k_heads = jnp.pad(f_k.reshape(d, N_HEADS, HEAD_DIM), ((0, 0), (0, 0), (0, LANES - HEAD_DIM))).reshape(d, -1)
    w_row = jnp.concatenate([c_kv, c_q, k_r, k_r_rot, pad_cols(f_l, LANES), f_k_heads,
                             pad_cols(s_k, LANES), pad_cols(i_k, LANES)], axis=1).astype(BF16)
    w_col = jnp.concatenate([f_q, f_v, s_q, s_v, i_q, pad_cols(i_w, 16), pad_cols(f_l, 16), gate], axis=1)
    w_col_t = w_col.T.astype(BF16)

    r = w_uq.shape[0]
    wq = w_uq.reshape(r, N_HEADS, MLA_QK)
    rope = wq[:, :, MLA_NOPE:]
    wq_rot = jnp.concatenate([jnp.zeros((r, N_HEADS, MLA_NOPE), w_uq.dtype), -rope[:, :, half:], rope[:, :, :half]], axis=2)
    wkv = w_ukv.reshape(w_ukv.shape[0], N_HEADS, MLA_NOPE + HEAD_DIM)
    wk = jnp.pad(wkv[:, :, :MLA_NOPE], ((0, 0), (0, 0), (0, LANES - MLA_NOPE))).reshape(wkv.shape[0], -1)
    wv = wkv[:, :, MLA_NOPE:].reshape(wkv.shape[0], -1)
    place = jnp.zeros((MLA_ROPE, N_HEADS, LANES), F32)
    place = place.at[jnp.arange(MLA_ROPE), :, MLA_NOPE + jnp.arange(MLA_ROPE)].set(1.0).reshape(MLA_ROPE, -1)
    return (w_row, w_col_t, q_norm.reshape(1, -1), kv_norm.reshape(1, -1),
            w_uq.T.astype(BF16), wq_rot.reshape(r, -1).T.astype(BF16), wk.astype(BF16), wv.T.astype(BF16),
            place.astype(BF16))


def _fox_placement():
    h = jnp.arange(N_HEADS)
    pk = jnp.zeros((3, FOX_PAD, N_HEADS * LANES), F32)
    pq = jnp.zeros((3, N_HEADS * FOX_PAD, FOX_PAD), F32)
    one_k = jnp.zeros((1, N_HEADS * LANES), F32)
    one_q = jnp.zeros((N_HEADS * FOX_PAD, 1), F32)
    for n in range(3):
        pk = pk.at[n, h, LANES * h + HEAD_DIM + 3 + n].set(-1.0)
        pq = pq.at[n, FOX_PAD * h + n, h].set(1.0)
        one_k = one_k.at[0, LANES * h + HEAD_DIM + n].set(1.0)
        one_q = one_q.at[FOX_PAD * h + 3 + n, 0].set(1.0)
    return pk.reshape(3 * FOX_PAD, N_HEADS * LANES).astype(BF16), one_k, pq.astype(BF16), one_q


def _alibi_placement(slopes):
    a = jnp.asarray(slopes, F32) * LOG2E
    neg_a = jnp.pad(-a, (0, FOX_PAD - N_HEADS)).reshape(FOX_PAD, 1)
    terms = jnp.stack([t.astype(F32) for t in _split3(a)], axis=1)
    a_q = jnp.zeros((N_HEADS, FOX_PAD), F32).at[:, 3:6].set(terms).at[:, 6:9].set(terms).reshape(-1, 1)
    e_k = jnp.zeros((3, LANES), F32)
    for n in range(3):
        e_k = e_k.at[n, HEAD_DIM + 3 * n:HEAD_DIM + 3 * n + 3].set(1.0)
    return neg_a, a_q, e_k


def _rope_tables(positions):
    half = MLA_ROPE // 2
    inv = ROPE_THETA ** (-jnp.arange(half, dtype=F32) / half)
    ang = positions.astype(F32)[..., None] * inv
    cos, sin = jnp.cos(ang), jnp.sin(ang)
    b, s = positions.shape
    q_scale = (MLA_QK ** -0.5) * LOG2E
    cos2 = jnp.concatenate([cos, cos], -1)
    sin2 = jnp.concatenate([sin, sin], -1)
    ones = jnp.ones((b, s, MLA_NOPE), F32)
    cq = jnp.tile(jnp.concatenate([ones, cos2], -1) * q_scale, (1, 1, N_HEADS))
    sq = jnp.tile(jnp.concatenate([0.0 * ones, sin2], -1) * q_scale, (1, 1, N_HEADS))
    return cq.transpose(0, 2, 1), sq.transpose(0, 2, 1), cos2, sin2


def kernel(x, positions, w_in, b_gate, b_forget, mla_q_norm, mla_kv_norm, mla_w_uq, mla_w_ukv, w_branch, w_out,
           ln1_g, ln1_b, ffn_w_up, ffn_conv_w, ffn_conv_b, ffn_w_down, ln2_g, ln2_b):
    bsz, seq, dm = x.shape
    depth = w_in.shape[0]
    d_ff = ffn_w_down.shape[1]
    tk = ATT_TK
    nk = seq // tk
    topk = min(DSA_MAX_TOPK, seq // 4)
    alpha = (2 * depth) ** 0.25
    slopes = tuple(2.0 ** (-8.0 * i / N_HEADS) for i in range(1, N_HEADS + 1))
    n_chunks = d_ff // FF_CHUNK
    chunks = lambda w: w.reshape(w.shape[0], n_chunks, FF_CHUNK).transpose(1, 0, 2)

    pos_row = positions.reshape(bsz, 1, seq)
    pos_col = positions.reshape(bsz, nk, tk, 1)
    tabs = _rope_tables(positions) + (pos_row.astype(F32), positions.reshape(bsz, seq, 1).astype(F32))
    linear_alibi = (jnp.all(positions[:, 1:] >= positions[:, :-1], axis=1)
                    & (jnp.min(positions, axis=1) >= 0) & (jnp.max(positions, axis=1) < POS_LIMIT)).astype(jnp.int32)

    xf = x
    xb = x.astype(BF16)
    for l in range(depth):
        (mq, mk, mv, fq, fk, fv, sq, sk, sv, iq, ik, iw, gates) = _proj(
            xb, tabs, _proj_weights(w_in[l], mla_q_norm[l], mla_kv_norm[l], mla_w_uq[l], mla_w_ukv[l]),
            b_forget[l], b_gate[l], slopes)
        o_mla = _attention(mq, mk, mv)
        o_fox = _attention(fq, fk, fv)
        o_dsa = _dsa(linear_alibi, sq, sk, sv, iq, ik, iw, pos_row, pos_col, topk, slopes)
        xf, xb = _merge(o_mla, o_fox, o_dsa, gates, xf,
                        w_branch[l].transpose(0, 2, 1).astype(BF16), w_out[l].astype(BF16),
                        ln1_g[l].reshape(1, -1), ln1_b[l].reshape(1, -1), alpha)
        w_up = ffn_w_up[l]
        xf, xb = _ffn(xf, chunks(w_up[:, :d_ff]).astype(BF16), chunks(w_up[:, d_ff:]).astype(BF16),
                      chunks(ffn_conv_w[l]), chunks(ffn_conv_b[l].reshape(1, -1)),
                      ffn_w_down[l].reshape(n_chunks, FF_CHUNK, dm).astype(BF16),
                      ln2_g[l].reshape(1, -1), ln2_b[l].reshape(1, -1), alpha, min(FFN_TM, seq))
    return xf
```
